```python
import math
import jax
import jax.numpy as jnp
from jax import lax
import numpy as np

D_MODEL = 1024
BATCH = 16
SEQ = 4096
DEPTH = 4

HEAD_DIM = 64
MIX_GROUP_WIDTH = D_MODEL // 2

NSA_HEADS = MIX_GROUP_WIDTH // HEAD_DIM
NSA_KV_GROUPS = 2
NSA_HEADS_PER_GROUP = NSA_HEADS // NSA_KV_GROUPS
NSA_BRANCHES = 3
NSA_CMP_LEN = 32
NSA_CMP_STRIDE = 16
NSA_SEL_LEN = 64
NSA_TOP_N = 16
NSA_WINDOW = 512
NSA_PHI_HIDDEN = 4 * HEAD_DIM
NSA_QBLOCK = 64
NSA_FORCE_SCORE = 1e6

CONV_CH = MIX_GROUP_WIDTH
CONV_WIDTH = 31

DIFF_HEADS = MIX_GROUP_WIDTH // (2 * HEAD_DIM)
DIFF_DH = HEAD_DIM

DIL_HEADS = MIX_GROUP_WIDTH // HEAD_DIM
DIL_PAIRS = ((128, 1), (512, 4), (2048, 16))

D_FF = 256 * ((8 * D_MODEL // 3 + 255) // 256)
FFN_CONV_WIDTH = 3

REL_BUCKETS = 32
REL_MAX_DIST = 2048
REL_COL_NSA = 0
REL_COL_DIFF = NSA_HEADS
REL_COL_DIL = NSA_HEADS + DIFF_HEADS
REL_HEADS = NSA_HEADS + DIFF_HEADS + DIL_HEADS

ATTN_QBLOCK = 128
NORM_EPS = 1e-6
NEG_INF = -1e30

EV_Q = NSA_HEADS * HEAD_DIM
EV_KV = NSA_BRANCHES * 2 * NSA_KV_GROUPS * HEAD_DIM
EV_GATE = NSA_HEADS * NSA_BRANCHES
EV_GLU = 2 * CONV_CH
EV_IN = EV_Q + EV_KV + EV_GATE + EV_GLU
OD_C = DIFF_HEADS * 2 * DIFF_DH
OD_D = DIL_HEADS * HEAD_DIM
OD_IN = 3 * OD_C + 3 * OD_D
MIX_OUT = 2 * MIX_GROUP_WIDTH
N_EVEN = (DEPTH + 1) // 2
N_ODD = DEPTH // 2

kernel_name = 'hybrid_nsa_conformer_diff_dilated_trunk'


def rmsnorm(x, g):
    xf = x.astype(jnp.float32)
    y = xf * lax.rsqrt(jnp.mean(xf * xf, axis=-1, keepdims=True) + NORM_EPS)
    return (y * g.astype(jnp.float32)).astype(x.dtype)


def layernorm(x, g, b):
    xf = x.astype(jnp.float32)
    mu = jnp.mean(xf, axis=-1, keepdims=True)
    var = jnp.mean(jnp.square(xf - mu), axis=-1, keepdims=True)
    y = (xf - mu) * lax.rsqrt(var + NORM_EPS) * g.astype(jnp.float32) + b.astype(jnp.float32)
    return y.astype(x.dtype)


def rel_bucket(dist):
    n = jnp.maximum(dist, 0)
    max_exact = REL_BUCKETS // 2
    nf = jnp.maximum(n, 1).astype(jnp.float32)
    large = max_exact + (jnp.log(nf / max_exact) / math.log(REL_MAX_DIST / max_exact)
                         * (REL_BUCKETS - max_exact)).astype(jnp.int32)
    large = jnp.minimum(large, REL_BUCKETS - 1)
    return jnp.where(n < max_exact, n, large)


def rel_bias(table, dist, col0, n_heads):
    return table[rel_bucket(dist), col0:col0 + n_heads].astype(jnp.float32)


def causal_dwconv(x, w, b):
    k = w.shape[0]
    y = lax.conv_general_dilated(x, w[:, None, :], window_strides=(1,), padding=[(k - 1, 0)],
                                 dimension_numbers=('NWC', 'WIO', 'NWC'),
                                 feature_group_count=x.shape[-1])
    return y + b


def nsa_attention(q, kv, gates, phi_pos, phi_w1, phi_w2, rel_table):
    B, S = q.shape[0], q.shape[1]
    G, HPG, DH, QB = NSA_KV_GROUPS, NSA_HEADS_PER_GROUP, HEAD_DIM, NSA_QBLOCK
    scale = DH ** -0.5
    f32 = jnp.float32
    n_cmp = (S - NSA_CMP_LEN) // NSA_CMP_STRIDE + 1
    cmp_idx = np.arange(n_cmp)[:, None] * NSA_CMP_STRIDE + np.arange(NSA_CMP_LEN)[None, :]
    cmp_end = jnp.asarray(cmp_idx[:, -1], jnp.int32)
    kv_c = kv[:, :, 0]
    blocks = kv_c[:, cmp_idx] + phi_pos.transpose(1, 0, 2)[:, :, None, :]
    flat = blocks.transpose(0, 1, 3, 4, 2, 5).reshape(B, n_cmp, 2, G, NSA_CMP_LEN * DH)
    hid = jax.nn.silu(jnp.einsum('bnegf,efh->bnegh', flat, phi_w1))
    cmp = jnp.einsum('bnegh,ehd->bnegd', hid, phi_w2)
    k_cmp, v_cmp = cmp[:, :, 0], cmp[:, :, 1]
    n_sel = S // NSA_SEL_LEN
    n_top = min(NSA_TOP_N, n_sel)
    overlap = (cmp_idx[:, :, None] // NSA_SEL_LEN == np.arange(n_sel)[None, None, :]).mean(axis=1).astype(np.float32)
    k_sel = kv[:, :, 1, 0].reshape(B, n_sel, NSA_SEL_LEN, G, DH).transpose(0, 3, 1, 2, 4)
    v_sel = kv[:, :, 1, 1].reshape(B, n_sel, NSA_SEL_LEN, G, DH).transpose(0, 3, 1, 2, 4)
    sel_tab = rel_table[:, REL_COL_NSA:REL_COL_NSA + NSA_HEADS].astype(f32).reshape(REL_BUCKETS, G, HPG)
    pad_w = ((0, 0), (NSA_WINDOW, 0), (0, 0), (0, 0))
    k_win = jnp.pad(kv[:, :, 2, 0], pad_w)
    v_win = jnp.pad(kv[:, :, 2, 1], pad_w)
    w_dist = NSA_WINDOW + np.arange(QB)[:, None] - np.arange(NSA_WINDOW + QB)[None, :]
    w_band = (w_dist >= 0) & (w_dist < NSA_WINDOW)
    w_bias = rel_bias(rel_table, jnp.asarray(w_dist), REL_COL_NSA, NSA_HEADS)
    w_bias = w_bias.transpose(2, 0, 1).reshape(G, HPG, QB, NSA_WINDOW + QB)

    n_qb = S // QB
    q_blocks = q.reshape(B, n_qb, QB, G, HPG, DH).transpose(1, 0, 2, 3, 4, 5)
    g_blocks = gates.reshape(B, n_qb, QB, G, HPG, 3).transpose(1, 0, 2, 3, 4, 5)
    starts = jnp.arange(n_qb, dtype=jnp.int32) * QB
    bi = jnp.arange(B)[:, None, None, None]
    gi = jnp.arange(G)[None, :, None, None]
    j_sel = jnp.arange(n_sel)

    def one_block(args):
        qb, gb, s0 = args
        t = s0 + jnp.arange(QB)
        s_c = jnp.einsum('bqghd,bngd->bghqn', qb, k_cmp).astype(f32) * scale
        valid_c = cmp_end[None, :] <= t[:, None]
        p_c = jax.nn.softmax(jnp.where(valid_c, s_c, NEG_INF), axis=-1) * jnp.any(valid_c, axis=-1)[:, None]
        o_c = jnp.einsum('bghqn,bngd->bqghd', p_c, v_cmp.astype(f32))
        imp = jnp.einsum('bghqn,nj->bgqj', p_c, overlap)
        cur = t // NSA_SEL_LEN
        forced = (j_sel[None, :] == 0) | (j_sel[None, :] == cur[:, None]) | (j_sel[None, :] == cur[:, None] - 1)
        imp = jnp.where(forced, NSA_FORCE_SCORE, imp)
        imp = jnp.where(j_sel[None, :] * NSA_SEL_LEN <= t[:, None], imp, NEG_INF)
        _, sel = lax.top_k(imp, n_top)
        kg = k_sel[bi, gi, sel]
        vg = v_sel[bi, gi, sel]
        pos = sel[..., None] * NSA_SEL_LEN + jnp.arange(NSA_SEL_LEN)
        dist = t[None, None, :, None, None] - pos
        bias_s = sel_tab[rel_bucket(dist), gi[..., None]].transpose(0, 1, 5, 2, 3, 4)
        s_s = jnp.einsum('bqghd,bgqnkd->bghqnk', qb, kg).astype(f32) * scale + bias_s
        s_s = jnp.where((dist >= 0)[:, :, None], s_s, NEG_INF)
        p_s = jax.nn.softmax(s_s.reshape(B, G, HPG, QB, n_top * NSA_SEL_LEN), axis=-1).reshape(s_s.shape)
        o_s = jnp.einsum('bghqnk,bgqnkd->bqghd', p_s, vg.astype(f32))
        kw = lax.dynamic_slice_in_dim(k_win, s0, NSA_WINDOW + QB, axis=1)
        vw = lax.dynamic_slice_in_dim(v_win, s0, NSA_WINDOW + QB, axis=1)
        kpos = s0 - NSA_WINDOW + jnp.arange(NSA_WINDOW + QB)
        valid_w = w_band & (kpos >= 0)[None, :]
        s_w = jnp.einsum('bqghd,bkgd->bghqk', qb, kw).astype(f32) * scale + w_bias
        p_w = jax.nn.softmax(jnp.where(valid_w, s_w, NEG_INF), axis=-1)
        o_w = jnp.einsum('bghqk,bkgd->bqghd', p_w, vw.astype(f32))
        o = gb[..., 0:1] * o_c + gb[..., 1:2] * o_s + gb[..., 2:3] * o_w
        return o.reshape(B, QB, G * HPG * DH)

    out = lax.map(one_block, (q_blocks, g_blocks, starts))
    return out.transpose(1, 0, 2, 3).reshape(B, S, NSA_HEADS * HEAD_DIM)


def nsa_conformer_mixer(h, w_in, w_out, phi_pos, phi_w1, phi_w2, dw_w, dw_b, ln_g, ln_b, rel_table):
    B, S, _ = h.shape
    proj = h @ w_in
    q, kv, gate_logits, glu = jnp.split(proj, [EV_Q, EV_Q + EV_KV, EV_Q + EV_KV + EV_GATE], axis=-1)
    q = q.reshape(B, S, NSA_KV_GROUPS, NSA_HEADS_PER_GROUP, HEAD_DIM)
    kv = kv.reshape(B, S, NSA_BRANCHES, 2, NSA_KV_GROUPS, HEAD_DIM)
    gates = jax.nn.sigmoid(gate_logits.astype(jnp.float32)).reshape(
        B, S, NSA_KV_GROUPS, NSA_HEADS_PER_GROUP, NSA_BRANCHES)
    o_a = nsa_attention(q, kv, gates, phi_pos, phi_w1, phi_w2, rel_table)
    a, g = jnp.split(glu, 2, axis=-1)
    u = causal_dwconv(a * jax.nn.sigmoid(g), dw_w, dw_b)
    u = jax.nn.silu(layernorm(u, ln_g, ln_b))
    return jnp.concatenate([o_a.astype(h.dtype), u], axis=-1) @ w_out


def diff_attention(q, k, v, lam_params, norm_g, lam_init, rel_table):
    B, S = q.shape[0], q.shape[1]
    f32 = jnp.float32
    QB = ATTN_QBLOCK
    lp = lam_params.astype(f32)
    lam = jnp.exp(jnp.sum(lp[0] * lp[1])) - jnp.exp(jnp.sum(lp[2] * lp[3])) + lam_init
    scale = DIFF_DH ** -0.5
    n_qb = S // QB
    q_blocks = q.reshape(B, n_qb, QB, DIFF_HEADS, 2, DIFF_DH).transpose(1, 0, 2, 3, 4, 5)
    starts = jnp.arange(n_qb, dtype=jnp.int32) * QB
    kpos = jnp.arange(S)
    v32 = v.astype(f32)

    def one_block(args):
        qb, s0 = args
        t = s0 + jnp.arange(QB)
        dist = t[:, None] - kpos[None, :]
        bias = rel_bias(rel_table, dist, REL_COL_DIFF, DIFF_HEADS).transpose(2, 0, 1)
        s = jnp.einsum('bqhmd,bkhmd->bhmqk', qb, k).astype(f32) * scale + bias[None, :, None]
        p = jax.nn.softmax(jnp.where(dist >= 0, s, NEG_INF), axis=-1)
        a = p[:, :, 0] - lam * p[:, :, 1]
        return jnp.einsum('bhqk,bkhe->bqhe', a, v32)

    o = lax.map(one_block, (q_blocks, starts))
    o = o.transpose(1, 0, 2, 3, 4).reshape(B, S, DIFF_HEADS, 2 * DIFF_DH)
    o = rmsnorm(o, norm_g) * (1.0 - lam_init)
    return o.reshape(B, S, DIFF_HEADS * 2 * DIFF_DH)


def dilated_attention(q, k, v, rel_table):
    B, S = q.shape[0], q.shape[1]
    f32 = jnp.float32
    HD, DH = DIL_HEADS, HEAD_DIM
    scale = DH ** -0.5
    outs, lses = [], []
    for window, dil in DIL_PAIRS:
        blk = window // dil
        span = blk * dil
        s_pad = -(-S // span) * span
        L = s_pad // dil
        nb = L // blk

        def to_blocks(a):
            a = jnp.pad(a, ((0, 0), (0, s_pad - S), (0, 0), (0, 0)))
            return a.reshape(B, L, dil, HD, DH).transpose(0, 2, 1, 3, 4).reshape(B, dil, nb, blk, HD, DH)

        def with_prev(a):
            prev = jnp.pad(a, ((0, 0), (0, 0), (1, 0), (0, 0), (0, 0), (0, 0)))[:, :, :-1]
            return jnp.concatenate([prev, a], axis=3)

        qb = to_blocks(q)
        kc = with_prev(to_blocks(k))
        vc = with_prev(to_blocks(v))
        ik = np.arange(2 * blk)
        sub_dist = blk + np.arange(blk)[:, None] - ik[None, :]
        band = (sub_dist >= 0) & (sub_dist <= blk)
        valid = band[None] & ~((np.arange(nb) == 0)[:, None, None] & (ik < blk)[None, None, :])
        bias = rel_bias(rel_table, jnp.asarray(sub_dist * dil), REL_COL_DIL, HD).transpose(2, 0, 1)
        s = jnp.einsum('brnqhd,brnkhd->brnhqk', qb, kc).astype(f32) * scale + bias
        s = jnp.where(valid[:, None], s, NEG_INF)
        lse = jax.nn.logsumexp(s, axis=-1, keepdims=True)
        o = jnp.einsum('brnhqk,brnkhd->brnqhd', jnp.exp(s - lse), vc.astype(f32))
        o = o.reshape(B, dil, L, HD, DH).transpose(0, 2, 1, 3, 4).reshape(B, s_pad, HD, DH)[:, :S]
        lse = lse[..., 0].transpose(0, 1, 2, 4, 3).reshape(B, dil, L, HD)
        lse = lse.transpose(0, 2, 1, 3).reshape(B, s_pad, HD)[:, :S]
        outs.append(o)
        lses.append(lse)
    w = jax.nn.softmax(jnp.stack(lses, axis=0), axis=0)
    o = jnp.sum(w[..., None] * jnp.stack(outs, axis=0), axis=0)
    return o.reshape(B, S, HD * DH)


def diff_dilated_mixer(h, w_in, w_out, lam_params, diff_g, lam_init, rel_table):
    B, S, _ = h.shape
    proj = h @ w_in
    cq, ck, cv, dq, dk, dv = jnp.split(
        proj, [OD_C, 2 * OD_C, 3 * OD_C, 3 * OD_C + OD_D, 3 * OD_C + 2 * OD_D], axis=-1)
    o_c = diff_attention(cq.reshape(B, S, DIFF_HEADS, 2, DIFF_DH),
                         ck.reshape(B, S, DIFF_HEADS, 2, DIFF_DH),
                         cv.reshape(B, S, DIFF_HEADS, 2 * DIFF_DH),
                         lam_params, diff_g, lam_init, rel_table)
    o_d = dilated_attention(dq.reshape(B, S, DIL_HEADS, HEAD_DIM),
                            dk.reshape(B, S, DIL_HEADS, HEAD_DIM),
                            dv.reshape(B, S, DIL_HEADS, HEAD_DIM), rel_table)
    return jnp.concatenate([o_c.astype(h.dtype), o_d.astype(h.dtype)], axis=-1) @ w_out


def conv_glu_ffn(h, w_gate, w_up, conv_w, conv_b, w_down):
    g = causal_dwconv(h @ w_gate, conv_w, conv_b)
    return (jax.nn.gelu(g, approximate=False) * (h @ w_up)) @ w_down


def setup_inputs(seed: int = 0) -> dict:
    key = jax.random.key(seed)
    ks = jax.random.split(key, 24)
    f32 = jnp.float32

    def nrm(k, shape, s):
        return jax.random.normal(k, shape, f32) * s

    NE, NO = N_EVEN, N_ODD
    return {
        'x': nrm(ks[0], (BATCH, SEQ, D_MODEL), 1.0),
        'rel_bias_table': nrm(ks[1], (REL_BUCKETS, REL_HEADS), 0.5),
        'mix_norm_g': 1.0 + nrm(ks[2], (DEPTH, D_MODEL), 0.02),
        'ffn_norm_g': 1.0 + nrm(ks[3], (DEPTH, D_MODEL), 0.02),
        'ffn_w_gate': nrm(ks[4], (DEPTH, D_MODEL, D_FF), D_MODEL ** -0.5),
        'ffn_w_up': nrm(ks[5], (DEPTH, D_MODEL, D_FF), D_MODEL ** -0.5),
        'ffn_conv_w': nrm(ks[6], (DEPTH, FFN_CONV_WIDTH, D_FF), FFN_CONV_WIDTH ** -0.5),
        'ffn_conv_b': nrm(ks[7], (DEPTH, D_FF), 0.01),
        'ffn_w_down': nrm(ks[8], (DEPTH, D_FF, D_MODEL), D_FF ** -0.5),
        'ev_w_in': nrm(ks[9], (NE, D_MODEL, EV_IN), D_MODEL ** -0.5),
        'ev_w_out': nrm(ks[10], (NE, MIX_OUT, D_MODEL), MIX_OUT ** -0.5),
        'nsa_phi_pos': nrm(ks[11], (NE, 2, NSA_CMP_LEN, HEAD_DIM), 0.1),
        'nsa_phi_w1': nrm(ks[12], (NE, 2, NSA_CMP_LEN * HEAD_DIM, NSA_PHI_HIDDEN), (NSA_CMP_LEN * HEAD_DIM) ** -0.5),
        'nsa_phi_w2': nrm(ks[13], (NE, 2, NSA_PHI_HIDDEN, HEAD_DIM), NSA_PHI_HIDDEN ** -0.5),
        'conv_dw_w': nrm(ks[14], (NE, CONV_WIDTH, CONV_CH), CONV_WIDTH ** -0.5),
        'conv_dw_b': nrm(ks[15], (NE, CONV_CH), 0.01),
        'conv_ln_g': 1.0 + nrm(ks[16], (NE, CONV_CH), 0.02),
        'conv_ln_b': nrm(ks[17], (NE, CONV_CH), 0.01),
        'od_w_in': nrm(ks[18], (NO, D_MODEL, OD_IN), D_MODEL ** -0.5),
        'od_w_out': nrm(ks[19], (NO, MIX_OUT, D_MODEL), MIX_OUT ** -0.5),
        'diff_lambda': nrm(ks[20], (NO, 4, DIFF_DH), 0.1),
        'diff_norm_g': 1.0 + nrm(ks[21], (NO, 2 * DIFF_DH), 0.02),
        'final_norm_g': 1.0 + nrm(ks[22], (D_MODEL,), 0.02),
    }


def reference(x, rel_bias_table, mix_norm_g, ffn_norm_g, ffn_w_gate, ffn_w_up, ffn_conv_w, ffn_conv_b,
              ffn_w_down, ev_w_in, ev_w_out, nsa_phi_pos, nsa_phi_w1, nsa_phi_w2, conv_dw_w, conv_dw_b,
              conv_ln_g, conv_ln_b, od_w_in, od_w_out, diff_lambda, diff_norm_g, final_norm_g):
    for layer in range(DEPTH):
        h = rmsnorm(x, mix_norm_g[layer])
        if layer % 2 == 0:
            e = layer // 2
            x = x + nsa_conformer_mixer(h, ev_w_in[e], ev_w_out[e], nsa_phi_pos[e], nsa_phi_w1[e],
                                        nsa_phi_w2[e], conv_dw_w[e], conv_dw_b[e], conv_ln_g[e],
                                        conv_ln_b[e], rel_bias_table)
        else:
            o = layer // 2
            lam_init = 0.8 - 0.6 * math.exp(-0.3 * layer)
            x = x + diff_dilated_mixer(h, od_w_in[o], od_w_out[o], diff_lambda[o], diff_norm_g[o],
                                       lam_init, rel_bias_table)
        h = rmsnorm(x, ffn_norm_g[layer])
        x = x + conv_glu_ffn(h, ffn_w_gate[layer], ffn_w_up[layer], ffn_conv_w[layer],
                             ffn_conv_b[layer], ffn_w_down[layer])
    return rmsnorm(x, final_norm_g)
```

```python
import functools
import math

import numpy as np
import jax
import jax.numpy as jnp
from jax import lax
from jax.experimental import pallas as pl
from jax.experimental.pallas import tpu as pltpu

F32, BF16 = jnp.float32, jnp.bfloat16

D_MODEL = 1024
HEAD_DIM = 64
HALF = 512
NSA_GROUPS = 2
NSA_HPG = 4
NSA_CMP_LEN = 32
NSA_CMP_STRIDE = 16
NSA_SEL_LEN = 64
NSA_TOP_N = 16
NSA_WINDOW = 512
NSA_FORCE = 1e6
CONV_WIDTH = 31
DIFF_HEADS = 4
DIL_HEADS = 8
DIL_PAIRS = ((128, 1), (512, 4), (2048, 16))
D_FF = 2816
REL_BUCKETS = 32
REL_MAX_DIST = 2048
REL_COL_NSA = 0
REL_COL_DIFF = 8
REL_COL_DIL = 12
EPS = 1e-6
NEG = -1e30

LANE = 128
TILE = 128
FF_CHUNK = 256
VMEM_LIMIT = 56 * 1024 * 1024

_Q0, _KVC0, _KSEL0, _VSEL0, _KWIN0, _VWIN0, _GATE0, _GLUA0, _GLUG0, _EV_COLS = (
    0, 1024, 1280, 1536, 1792, 2048, 2304, 2560, 3072, 3584)


def _cparams(n_axes):
    return pltpu.CompilerParams(dimension_semantics=("arbitrary",) * n_axes,
                                vmem_limit_bytes=VMEM_LIMIT)


def _resident(shape, index_map):
    return pl.BlockSpec(shape, index_map, pipeline_mode=pl.Buffered(1))


def _nt(a, b):
    return lax.dot_general(a, b, (((1,), (1,)), ((), ())), preferred_element_type=F32)


def _rms_bf16(x, g):
    ms = jnp.mean(x * x, axis=-1, keepdims=True)
    return (x * lax.rsqrt(ms + EPS) * g).astype(BF16)


def _rel_bucket(dist):
    n = jnp.maximum(dist, 0)
    max_exact = REL_BUCKETS // 2
    nf = jnp.maximum(n, 1).astype(F32)
    large = max_exact + (jnp.log(nf / max_exact) / math.log(REL_MAX_DIST / max_exact)
                         * (REL_BUCKETS - max_exact)).astype(jnp.int32)
    large = jnp.minimum(large, REL_BUCKETS - 1)
    return jnp.where(n < max_exact, n, large)


def _bias_tiles(table, col0, n_heads, dist, valid):
    vals = table[_rel_bucket(jnp.asarray(dist, jnp.int32)), col0:col0 + n_heads].astype(F32)
    vals = jnp.where(jnp.asarray(valid)[..., None], vals, NEG)
    return jnp.transpose(vals, (3, 0, 1, 2))


def _causal_tiles(table, col0, n_heads, n_q):
    i = np.arange(TILE)[:, None] - np.arange(TILE)[None, :]
    dist = np.stack([np.full((TILE, TILE), -1)] + [TILE * d + i for d in range(n_q)])
    return _bias_tiles(table, col0, n_heads, dist, dist >= 0)


def _window_tiles(table, col0, n_heads):
    i = np.arange(TILE)[:, None] - np.arange(TILE)[None, :]
    dist = np.stack([NSA_WINDOW - TILE * b + i for b in range(NSA_WINDOW // TILE + 1)])
    return _bias_tiles(table, col0, n_heads, dist, (dist >= 0) & (dist < NSA_WINDOW))


def _dilated_tiles(table, col0, n_heads, dil):
    i = np.arange(TILE)[:, None] - np.arange(TILE)[None, :]
    sub = np.stack([TILE + i, i])
    return _bias_tiles(table, col0, n_heads, sub * dil, (sub >= 0) & (sub <= TILE))


def _even_proj_kernel(x_ref, g_ref, w_ref, q_ref, kvc_ref, ksel_ref, vsel_ref, kwin_ref, vwin_ref,
                      gate_ref, glu_ref, *, tm):
    h = _rms_bf16(x_ref[0], g_ref[...])

    def mm(a, b):
        return jnp.dot(h, w_ref[:, a:b], preferred_element_type=F32)

    q_ref[0] = (mm(_Q0, _KVC0) * (HEAD_DIM ** -0.5)).astype(BF16)
    kvc_ref[0] = mm(_KVC0, _KSEL0)
    ks = mm(_KSEL0, _VSEL0)
    lane = lax.broadcasted_iota(jnp.int32, ks.shape, 1) % LANE
    pos = pl.program_id(1) * tm + lax.broadcasted_iota(jnp.int32, ks.shape, 0)
    onehot = jnp.where(lane - HEAD_DIM == pos // NSA_SEL_LEN, 1.0, 0.0)
    ksel_ref[0] = jnp.where(lane >= HEAD_DIM, onehot, ks).astype(BF16)
    vsel_ref[0] = mm(_VSEL0, _KWIN0).astype(BF16)
    kwin_ref[0] = mm(_KWIN0, _VWIN0).astype(BF16)
    vwin_ref[0] = mm(_VWIN0, _GATE0).astype(BF16)
    gate_ref[0] = jax.nn.sigmoid(mm(_GATE0, _GLUA0))
    glu_ref[0] = mm(_GLUA0, _GLUG0) * jax.nn.sigmoid(mm(_GLUG0, _EV_COLS))


def _even_proj(x, g, w, tm):
    B, S, D = x.shape
    row = lambda n: pl.BlockSpec((1, tm, n), lambda b, i: (b, i, 0))
    outs = [(1024, BF16), (256, F32), (256, BF16), (256, BF16), (256, BF16), (256, BF16), (256, F32), (512, F32)]
    return pl.pallas_call(
        functools.partial(_even_proj_kernel, tm=tm),
        grid=(B, S // tm),
        in_specs=[row(D), _resident((1, D), lambda b, i: (0, 0)), _resident((D, _EV_COLS), lambda b, i: (0, 0))],
        out_specs=[row(n) for n, _ in outs],
        out_shape=[jax.ShapeDtypeStruct((B, S, n), dt) for n, dt in outs],
        compiler_params=_cparams(2),
        name="even_proj",
    )(x, g, w)


def _even_proj_weights(w_in):
    D = w_in.shape[0]

    def slots(cols, n, width):
        p = cols.reshape(D, n, width)
        return jnp.pad(p, ((0, 0), (0, 0), (0, LANE - width))).reshape(D, n * LANE)

    kv0 = HALF
    parts = [slots(w_in[:, :HALF], 8, HEAD_DIM),
             w_in[:, kv0:kv0 + 256]]
    for br in (1, 2):
        for e in (0, 1):
            c0 = kv0 + br * 256 + e * 128
            parts.append(slots(w_in[:, c0:c0 + 128], 2, HEAD_DIM))
    g0 = kv0 + 768
    parts.append(slots(w_in[:, g0:g0 + 24], 2, 12))
    parts.append(w_in[:, g0 + 24:])
    return jnp.concatenate(parts, axis=1).astype(BF16)


def _compress_kernel(r_ref, pos_ref, w1_ref, w2_ref, o_ref):
    r = r_ref[0, 0]
    nc = r.shape[0]
    half = NSA_CMP_STRIDE * HEAD_DIM
    pos = pos_ref[0]
    lo = (r + pos[:, :half]).astype(BF16)
    hi = (pltpu.roll(r, nc - 1, 0) + pos[:, half:]).astype(BF16)
    hid = (jnp.dot(lo, w1_ref[0, :half, :], preferred_element_type=F32)
           + jnp.dot(hi, w1_ref[0, half:, :], preferred_element_type=F32))
    hid = hid * jax.nn.sigmoid(hid)
    o_ref[0, 0] = jnp.dot(hid.astype(BF16), w2_ref[0], preferred_element_type=F32).astype(BF16)


def _compress(r, pos, w1, w2):
    B, _, NC, W = r.shape
    return pl.pallas_call(
        _compress_kernel,
        grid=(B, 4),
        in_specs=[pl.BlockSpec((1, 1, NC, W), lambda b, c: (b, c, 0, 0)),
                  pl.BlockSpec((1, 1, 2 * W), lambda b, c: (c // 2, 0, 0)),
                  pl.BlockSpec((1, 2 * W, 256), lambda b, c: (c // 2, 0, 0)),
                  pl.BlockSpec((1, 256, LANE), lambda b, c: (c // 2, 0, 0))],
        out_specs=pl.BlockSpec((1, 1, NC, LANE), lambda b, c: (b, c, 0, 0)),
        out_shape=jax.ShapeDtypeStruct((B, 4, NC, LANE), BF16),
        compiler_params=_cparams(2),
        name="nsa_compress",
    )(r, pos, w1, w2)


def _stack_heads(x, n):
    return jnp.concatenate([x] * n, axis=0)


def _nsa_kernel(q_ref, ksel_ref, vsel_ref, kwin_ref, vwin_ref, kcmp_ref, vcmp_ref, gate_ref, ovt_ref,
                tsel_ref, twin_ref, o_ref, acc_ref, *, tk, n_cmp, n_top):
    qi = pl.program_id(2)
    s0 = qi * TILE
    H = NSA_HPG
    qs = jnp.concatenate([q_ref[0, :, LANE * h:LANE * (h + 1)] for h in range(H)], axis=0)

    nc = kcmp_ref.shape[2]
    s_c = _nt(qs, kcmp_ref[0, 0])
    t_row = s0 + lax.broadcasted_iota(jnp.int32, (TILE, nc), 0)
    n_idx = lax.broadcasted_iota(jnp.int32, (TILE, nc), 1)
    ok = jnp.where(n_idx * NSA_CMP_STRIDE + (NSA_CMP_LEN - 1) <= t_row, 1.0, 0.0)
    ok = _stack_heads(jnp.where(n_idx < n_cmp, ok, 0.0), H)
    s_c = jnp.where(ok > 0, s_c, NEG)
    p_c = jnp.exp(s_c - jnp.max(s_c, axis=-1, keepdims=True)) * ok
    l_c = jnp.sum(p_c, axis=-1, keepdims=True)
    p_c = p_c * jnp.where(l_c > 0, 1.0 / l_c, 0.0)
    o_c = jnp.dot(p_c.astype(BF16), vcmp_ref[0, 0], preferred_element_type=F32)

    p_cat = jnp.concatenate([p_c[TILE * h:TILE * (h + 1)] for h in range(H)], axis=1).astype(BF16)
    imp = _nt(ovt_ref[...], p_cat)[HEAD_DIM:]
    nj = imp.shape[0]
    j = lax.broadcasted_iota(jnp.int32, (nj, TILE), 0)
    t = s0 + lax.broadcasted_iota(jnp.int32, (nj, TILE), 1)
    cur = t // NSA_SEL_LEN
    forced = jnp.where(j == 0, 1.0, 0.0) + jnp.where(j == cur, 1.0, 0.0) + jnp.where(j == cur - 1, 1.0, 0.0)
    imp = jnp.where(forced > 0, NSA_FORCE, imp)
    imp = jnp.where(j * NSA_SEL_LEN <= t, imp, NEG)
    rank = jnp.zeros((nj, TILE), F32)
    for k in range(nj):
        vk = imp[k:k + 1, :]
        rank = rank + jnp.where(j > k, jnp.where(vk >= imp, 1.0, 0.0), jnp.where(vk > imp, 1.0, 0.0))
    mneg = jnp.where(rank < n_top, 0.0, NEG)
    m_t = jnp.concatenate([jnp.zeros((LANE - nj, TILE), F32), mneg], axis=0)
    q_aug = qs + _stack_heads(m_t.T.astype(BF16), H)

    sub = tk // TILE

    def body(kc, carry):
        m, l = carry
        k0 = pl.multiple_of(kc * tk, tk)
        s = _nt(q_aug, ksel_ref[0, pl.ds(k0, tk), :])
        rows = []
        for h in range(H):
            cols = [tsel_ref[h, jnp.maximum(qi - (kc * sub + c) + 1, 0)] for c in range(sub)]
            rows.append(jnp.concatenate(cols, axis=1))
        s = s + jnp.concatenate(rows, axis=0)
        m_new = jnp.maximum(m, jnp.max(s, axis=-1, keepdims=True))
        alpha = jnp.exp(m - m_new)
        p = jnp.exp(s - m_new)
        l = alpha * l + jnp.sum(p, axis=-1, keepdims=True)
        acc_ref[...] = acc_ref[...] * alpha + jnp.dot(
            p.astype(BF16), vsel_ref[0, pl.ds(k0, tk), :], preferred_element_type=F32)
        return m_new, l

    acc_ref[...] = jnp.zeros_like(acc_ref)
    _, l_s = lax.fori_loop(0, qi // sub + 1, body,
                           (jnp.full((H * TILE, 1), NEG, F32), jnp.zeros((H * TILE, 1), F32)))
    o_s = acc_ref[...] / l_s

    wlen = NSA_WINDOW + TILE
    s_w = _nt(qs, kwin_ref[0, pl.ds(pl.multiple_of(s0, TILE), wlen), :])
    nb = wlen // TILE
    s_w = s_w + jnp.concatenate(
        [jnp.concatenate([twin_ref[h, b] for b in range(nb)], axis=1) for h in range(H)], axis=0)
    kpos = s0 - NSA_WINDOW + lax.broadcasted_iota(jnp.int32, (1, wlen), 1)
    s_w = s_w + jnp.where(kpos >= 0, 0.0, NEG)
    p_w = jnp.exp(s_w - jnp.max(s_w, axis=-1, keepdims=True))
    l_w = jnp.sum(p_w, axis=-1, keepdims=True)
    o_w = jnp.dot(p_w.astype(BF16), vwin_ref[0, pl.ds(pl.multiple_of(s0, TILE), wlen), :],
                  preferred_element_type=F32) / l_w

    gt = gate_ref[0]

    def gate(br):
        return jnp.concatenate([gt[:, 3 * h + br:3 * h + br + 1] for h in range(H)], axis=0)

    o = gate(0) * o_c + gate(1) * o_s + gate(2) * o_w
    o_ref[0] = jnp.concatenate([o[TILE * h:TILE * (h + 1), :HEAD_DIM] for h in range(H)], axis=1).astype(BF16)


def _nsa_attention(q, ksel, vsel, kwin, vwin, cmp, gates, ovt, tsel, twin, tk):
    B, S, _ = q.shape
    NC = cmp.shape[2]
    n_q = S // TILE
    G = NSA_GROUPS
    kv = lambda rows: pl.BlockSpec((1, rows, LANE), lambda g, b, i: (b, 0, g))
    return pl.pallas_call(
        functools.partial(_nsa_kernel, tk=tk, n_cmp=S // NSA_CMP_STRIDE - 1,
                          n_top=min(NSA_TOP_N, S // NSA_SEL_LEN)),
        grid=(G, B, n_q),
        in_specs=[pl.BlockSpec((1, TILE, NSA_HPG * LANE), lambda g, b, i: (b, i, g)),
                  kv(S), kv(S), kv(S + NSA_WINDOW), kv(S + NSA_WINDOW),
                  pl.BlockSpec((1, 1, NC, LANE), lambda g, b, i: (b, g, 0, 0)),
                  pl.BlockSpec((1, 1, NC, LANE), lambda g, b, i: (b, G + g, 0, 0)),
                  pl.BlockSpec((1, TILE, LANE), lambda g, b, i: (b, i, g)),
                  _resident(ovt.shape, lambda g, b, i: (0, 0)),
                  _resident((NSA_HPG,) + tsel.shape[1:], lambda g, b, i: (g, 0, 0, 0)),
                  _resident((NSA_HPG,) + twin.shape[1:], lambda g, b, i: (g, 0, 0, 0))],
        out_specs=pl.BlockSpec((1, TILE, NSA_HPG * HEAD_DIM), lambda g, b, i: (b, i, g)),
        out_shape=jax.ShapeDtypeStruct((B, S, HALF), BF16),
        scratch_shapes=[pltpu.VMEM((NSA_HPG * TILE, LANE), F32)],
        compiler_params=_cparams(3),
        name="nsa_attention",
    )(q, ksel, vsel, kwin, vwin, cmp, cmp, gates, ovt, tsel, twin)


def _overlap_t(S):
    n_cmp = (S - NSA_CMP_LEN) // NSA_CMP_STRIDE + 1
    n_sel = S // NSA_SEL_LEN
    cmp_idx = np.arange(n_cmp)[:, None] * NSA_CMP_STRIDE + np.arange(NSA_CMP_LEN)[None, :]
    ov = (cmp_idx[:, :, None] // NSA_SEL_LEN == np.arange(n_sel)[None, None, :]).mean(axis=1)
    out = np.zeros((LANE, S // NSA_CMP_STRIDE), np.float32)
    out[HEAD_DIM:HEAD_DIM + n_sel, :n_cmp] = ov.T
    return jnp.asarray(np.tile(out, (1, NSA_HPG)), BF16)


def _conformer_kernel(x_ref, w_ref, b_ref, g_ref, beta_ref, o_ref, buf, *, tm):
    halo = 32

    @pl.when(pl.program_id(1) == 0)
    def _():
        buf[0:halo] = jnp.zeros((halo, buf.shape[1]), F32)

    buf[halo:halo + tm] = x_ref[0]
    acc = jnp.zeros((tm, buf.shape[1]), F32) + b_ref[...]
    for j in range(CONV_WIDTH):
        acc = acc + w_ref[j:j + 1, :] * buf[pl.ds(halo - (CONV_WIDTH - 1) + j, tm), :]
    mu = jnp.mean(acc, axis=-1, keepdims=True)
    var = jnp.mean(jnp.square(acc - mu), axis=-1, keepdims=True)
    y = (acc - mu) * lax.rsqrt(var + EPS) * g_ref[...] + beta_ref[...]
    o_ref[0] = (y * jax.nn.sigmoid(y)).astype(BF16)
    buf[0:halo] = buf[tm:tm + halo]


def _conformer(glu, w, b, g, beta, tm):
    B, S, C = glu.shape
    vec = lambda r: pl.BlockSpec((r, C), lambda bb, i: (0, 0))
    return pl.pallas_call(
        functools.partial(_conformer_kernel, tm=tm),
        grid=(B, S // tm),
        in_specs=[pl.BlockSpec((1, tm, C), lambda bb, i: (bb, i, 0)), vec(CONV_WIDTH), vec(1), vec(1), vec(1)],
        out_specs=pl.BlockSpec((1, tm, C), lambda bb, i: (bb, i, 0)),
        out_shape=jax.ShapeDtypeStruct((B, S, C), BF16),
        scratch_shapes=[pltpu.VMEM((tm + 32, C), F32)],
        compiler_params=_cparams(2),
        name="conformer_conv",
    )(glu, w, b, g, beta)


def _even_out_kernel(x_ref, a_ref, u_ref, w_ref, o_ref):
    o_ref[0] = (x_ref[0] + jnp.dot(a_ref[0], w_ref[:HALF, :], preferred_element_type=F32)
                + jnp.dot(u_ref[0], w_ref[HALF:, :], preferred_element_type=F32))


def _even_out(x, a, u, w, tm):
    B, S, D = x.shape
    row = lambda n: pl.BlockSpec((1, tm, n), lambda b, i: (b, i, 0))
    return pl.pallas_call(
        _even_out_kernel, grid=(B, S // tm),
        in_specs=[row(D), row(HALF), row(HALF), _resident((D, D), lambda b, i: (0, 0))],
        out_specs=row(D), out_shape=jax.ShapeDtypeStruct((B, S, D), F32),
        compiler_params=_cparams(2), name="even_out",
    )(x, a, u, w)


def _odd_out_kernel(x_ref, c_ref, o1_ref, o2_ref, o3_ref, l1_ref, l2_ref, l3_ref, w_ref, o_ref):
    l1, l2, l3 = l1_ref[0], l2_ref[0], l3_ref[0]
    m = jnp.maximum(jnp.maximum(l1, l2), l3)
    e1, e2, e3 = jnp.exp(l1 - m), jnp.exp(l2 - m), jnp.exp(l3 - m)
    od = (e1 * o1_ref[0] + e2 * o2_ref[0] + e3 * o3_ref[0]) / (e1 + e2 + e3)
    o_ref[0] = (x_ref[0] + jnp.dot(c_ref[0], w_ref[:HALF, :], preferred_element_type=F32)
                + jnp.dot(od.astype(BF16), w_ref[HALF:, :], preferred_element_type=F32))


def _odd_out(x, oc, os_, ls, w, tm):
    B, S, D = x.shape
    row = lambda n: pl.BlockSpec((1, tm, n), lambda b, i: (b, i, 0))
    return pl.pallas_call(
        _odd_out_kernel, grid=(B, S // tm),
        in_specs=[row(D)] + [row(HALF)] * 7 + [_resident((D, D), lambda b, i: (0, 0))],
        out_specs=row(D), out_shape=jax.ShapeDtypeStruct((B, S, D), F32),
        compiler_params=_cparams(2), name="odd_out",
    )(x, oc, *os_, *ls, w)


def _odd_proj_kernel(x_ref, g_ref, w_ref, cq_ref, ck_ref, cv_ref, dq_ref, dk_ref, dv_ref):
    h = _rms_bf16(x_ref[0], g_ref[...])
    scale = HEAD_DIM ** -0.5
    for n, (ref, s) in enumerate(((cq_ref, scale), (ck_ref, 1.0), (cv_ref, 1.0),
                                  (dq_ref, scale), (dk_ref, 1.0), (dv_ref, 1.0))):
        y = jnp.dot(h, w_ref[:, HALF * n:HALF * (n + 1)], preferred_element_type=F32)
        ref[0] = (y * s).astype(BF16)


def _odd_proj(x, g, w, tm):
    B, S, D = x.shape
    row = lambda n: pl.BlockSpec((1, tm, n), lambda b, i: (b, i, 0))
    return pl.pallas_call(
        _odd_proj_kernel, grid=(B, S // tm),
        in_specs=[row(D), _resident((1, D), lambda b, i: (0, 0)), _resident((D, 6 * HALF), lambda b, i: (0, 0))],
        out_specs=[row(HALF)] * 6,
        out_shape=[jax.ShapeDtypeStruct((B, S, HALF), BF16)] * 6,
        compiler_params=_cparams(2), name="odd_proj",
    )(x, g, w)


def _diff_kernel(lp_ref, q_ref, k_ref, v_ref, t_ref, g_ref, o_ref, acc_ref, *, tk, lam_init):
    qi = pl.program_id(2)
    lp = lp_ref[...]
    lam = (jnp.exp(jnp.sum(lp[0:1] * lp[1:2], axis=-1, keepdims=True))
           - jnp.exp(jnp.sum(lp[2:3] * lp[3:4], axis=-1, keepdims=True)) + lam_init)
    q = q_ref[0]
    lane = lax.broadcasted_iota(jnp.int32, q.shape, 1)
    zero = jnp.zeros_like(q)
    qs = jnp.concatenate([jnp.where(lane < HEAD_DIM, q, zero), jnp.where(lane >= HEAD_DIM, q, zero)], axis=0)
    sub = tk // TILE

    def body(kc, carry):
        m, l = carry
        k0 = pl.multiple_of(kc * tk, tk)
        s = _nt(qs, k_ref[0, pl.ds(k0, tk), :])
        bt = jnp.concatenate([t_ref[0, jnp.maximum(qi - (kc * sub + c) + 1, 0)] for c in range(sub)], axis=1)
        s = s + _stack_heads(bt, 2)
        m_new = jnp.maximum(m, jnp.max(s, axis=-1, keepdims=True))
        alpha = jnp.exp(m - m_new)
        p = jnp.exp(s - m_new)
        l = alpha * l + jnp.sum(p, axis=-1, keepdims=True)
        acc_ref[...] = acc_ref[...] * alpha + jnp.dot(
            p.astype(BF16), v_ref[0, pl.ds(k0, tk), :], preferred_element_type=F32)
        return m_new, l

    acc_ref[...] = jnp.zeros_like(acc_ref)
    _, l = lax.fori_loop(0, qi // sub + 1, body,
                         (jnp.full((2 * TILE, 1), NEG, F32), jnp.zeros((2 * TILE, 1), F32)))
    on = acc_ref[...] / l
    o = on[:TILE] - lam * on[TILE:]
    o = o * lax.rsqrt(jnp.mean(o * o, axis=-1, keepdims=True) + EPS) * g_ref[...] * (1.0 - lam_init)
    o_ref[0] = o.astype(BF16)


def _diff_attention(lp, q, k, v, tiles, g, lam_init, tk):
    B, S, _ = q.shape
    n_q = S // TILE
    return pl.pallas_call(
        functools.partial(_diff_kernel, tk=tk, lam_init=lam_init),
        grid=(DIFF_HEADS, B, n_q),
        in_specs=[pl.BlockSpec(lp.shape, lambda h, b, i: (0, 0)),
                  pl.BlockSpec((1, TILE, LANE), lambda h, b, i: (b, i, h)),
                  pl.BlockSpec((1, S, LANE), lambda h, b, i: (b, 0, h)),
                  pl.BlockSpec((1, S, LANE), lambda h, b, i: (b, 0, h)),
                  _resident((1,) + tiles.shape[1:], lambda h, b, i: (h, 0, 0, 0)),
                  pl.BlockSpec(g.shape, lambda h, b, i: (0, 0))],
        out_specs=pl.BlockSpec((1, TILE, LANE), lambda h, b, i: (b, i, h)),
        out_shape=jax.ShapeDtypeStruct((B, S, HALF), BF16),
        scratch_shapes=[pltpu.VMEM((2 * TILE, LANE), F32)],
        compiler_params=_cparams(3), name="diff_attention",
    )(lp, q, k, v, tiles, g)


def _dilated_kernel(q_ref, k_ref, v_ref, t_ref, o_ref, lse_ref):
    n = pl.program_id(2)
    H = 4
    q = q_ref[0]
    lane = lax.broadcasted_iota(jnp.int32, q.shape, 1) // HEAD_DIM
    zero = jnp.zeros_like(q)
    qs = jnp.concatenate([jnp.where(lane == h, q, zero) for h in range(H)], axis=0)
    cur = pl.multiple_of(n * TILE, TILE)
    prev = pl.multiple_of(jnp.maximum(n - 1, 0) * TILE, TILE)
    kcat = jnp.concatenate([k_ref[0, pl.ds(prev, TILE), :], k_ref[0, pl.ds(cur, TILE), :]], axis=0)
    vcat = jnp.concatenate([v_ref[0, pl.ds(prev, TILE), :], v_ref[0, pl.ds(cur, TILE), :]], axis=0)
    s = _nt(qs, kcat)
    s = s + jnp.concatenate([jnp.concatenate([t_ref[h, 0], t_ref[h, 1]], axis=1) for h in range(H)], axis=0)
    col = lax.broadcasted_iota(jnp.int32, (1, 2 * TILE), 1)
    s = s + jnp.where((col < TILE) & (n == 0), NEG, 0.0)
    m = jnp.max(s, axis=-1, keepdims=True)
    p = jnp.exp(s - m)
    l = jnp.sum(p, axis=-1, keepdims=True)
    r = jnp.dot(p.astype(BF16), vcat, preferred_element_type=F32) / l
    lse = m + jnp.log(l)
    lane_f = lax.broadcasted_iota(jnp.int32, (TILE, q.shape[1]), 1) // HEAD_DIM
    out = jnp.zeros((TILE, q.shape[1]), F32)
    lse_out = jnp.zeros((TILE, q.shape[1]), F32)
    for h in range(H):
        out = jnp.where(lane_f == h, r[TILE * h:TILE * (h + 1)], out)
        lse_out = jnp.where(lane_f == h, lse[TILE * h:TILE * (h + 1)], lse_out)
    o_ref[0] = out.astype(BF16)
    lse_ref[0] = lse_out


def _dilated_group(q, k, v, tiles, dil):
    B, S, C = q.shape
    L = S // dil
    W = 4 * HEAD_DIM
    view = lambda a: a.reshape(B, L, dil * C)
    qb = pl.BlockSpec((1, TILE, W), lambda lb, b, n: (b, n, lb))
    kvb = pl.BlockSpec((1, L, W), lambda lb, b, n: (b, 0, lb))
    o, lse = pl.pallas_call(
        _dilated_kernel,
        grid=(dil * C // W, B, L // TILE),
        in_specs=[qb, kvb, kvb, _resident((4,) + tiles.shape[1:], lambda lb, b, n: (lb % 2, 0, 0, 0))],
        out_specs=[qb, qb],
        out_shape=[jax.ShapeDtypeStruct((B, L, dil * C), BF16), jax.ShapeDtypeStruct((B, L, dil * C), F32)],
        compiler_params=_cparams(3), name=f"dilated_attention_d{dil}",
    )(view(q), view(k), view(v), tiles)
    return o.reshape(B, S, C), lse.reshape(B, S, C)


def _ffn_kernel(x_ref, g_ref, wg_ref, wu_ref, cw_ref, cb_ref, wd_ref, fg_ref, o_ref, gbuf, halo, *, tm, final):
    x = x_ref[0]
    h = _rms_bf16(x, g_ref[...])

    @pl.when(pl.program_id(1) == 0)
    def _():
        halo[...] = jnp.zeros_like(halo)

    o_ref[0] = x
    for c in range(D_FF // FF_CHUNK):
        cs = slice(FF_CHUNK * c, FF_CHUNK * (c + 1))
        gg = jnp.dot(h, wg_ref[:, cs], preferred_element_type=F32)
        gbuf[0:8] = halo[c]
        gbuf[8:8 + tm] = gg
        halo[c] = gg[tm - 8:tm]
        conv = (cw_ref[0:1, cs] * gbuf[pl.ds(6, tm), :] + cw_ref[1:2, cs] * gbuf[pl.ds(7, tm), :]
                + cw_ref[2:3, cs] * gg + cb_ref[:, cs])
        act = 0.5 * conv * (1.0 + lax.erf(conv * (2.0 ** -0.5)))
        act = act * jnp.dot(h, wu_ref[:, cs], preferred_element_type=F32)
        o_ref[0] += jnp.dot(act.astype(BF16), wd_ref[cs, :], preferred_element_type=F32)
    if final:
        y = o_ref[0]
        o_ref[0] = y * lax.rsqrt(jnp.mean(y * y, axis=-1, keepdims=True) + EPS) * fg_ref[...]


def _ffn(x, g, wg, wu, cw, cb, wd, fg, tm, final):
    B, S, D = x.shape
    row = pl.BlockSpec((1, tm, D), lambda b, i: (b, i, 0))
    const = lambda shape: _resident(shape, lambda b, i: (0,) * len(shape))
    return pl.pallas_call(
        functools.partial(_ffn_kernel, tm=tm, final=final),
        grid=(B, S // tm),
        in_specs=[row, const((1, D)), const((D, D_FF)), const((D, D_FF)), const(cw.shape), const((1, D_FF)),
                  const((D_FF, D)), const((1, D))],
        out_specs=row, out_shape=jax.ShapeDtypeStruct((B, S, D), F32),
        scratch_shapes=[pltpu.VMEM((tm + 8, FF_CHUNK), F32), pltpu.VMEM((D_FF // FF_CHUNK, 8, FF_CHUNK), F32)],
        compiler_params=_cparams(2), name="conv_glu_ffn",
    )(x, g, wg, wu, cw, cb, wd, fg)


def kernel(x, rel_bias_table, mix_norm_g, ffn_norm_g, ffn_w_gate, ffn_w_up, ffn_conv_w, ffn_conv_b, ffn_w_down,
           ev_w_in, ev_w_out, nsa_phi_pos, nsa_phi_w1, nsa_phi_w2, conv_dw_w, conv_dw_b, conv_ln_g, conv_ln_b,
           od_w_in, od_w_out, diff_lambda, diff_norm_g, final_norm_g):
    B, S, D = x.shape
    depth = mix_norm_g.shape[0]
    assert D == D_MODEL and S % 2048 == 0 and S // NSA_SEL_LEN <= HEAD_DIM
    tm = 512
    tk = 256
    n_q = S // TILE
    NC = S // NSA_CMP_STRIDE

    tsel = _causal_tiles(rel_bias_table, REL_COL_NSA, NSA_GROUPS * NSA_HPG, n_q)
    twin = _window_tiles(rel_bias_table, REL_COL_NSA, NSA_GROUPS * NSA_HPG)
    tdiff = _causal_tiles(rel_bias_table, REL_COL_DIFF, DIFF_HEADS, n_q)
    tdil = [_dilated_tiles(rel_bias_table, REL_COL_DIL, DIL_HEADS, dil) for _, dil in DIL_PAIRS]
    ovt = _overlap_t(S)

    for layer in range(depth):
        g_mix = mix_norm_g[layer][None, :]
        if layer % 2 == 0:
            e = layer // 2
            q, kvc, ksel, vsel, kwin, vwin, gates, glu = _even_proj(x, g_mix, _even_proj_weights(ev_w_in[e]), tm)
            r = kvc.reshape(B, NC, NSA_CMP_STRIDE, 4, HEAD_DIM).transpose(0, 3, 1, 2, 4)
            r = r.reshape(B, 4, NC, NSA_CMP_STRIDE * HEAD_DIM)
            w2 = jnp.pad(nsa_phi_w2[e], ((0, 0), (0, 0), (0, LANE - HEAD_DIM))).astype(BF16)
            cmp = _compress(r, nsa_phi_pos[e].reshape(2, 1, NSA_CMP_LEN * HEAD_DIM), nsa_phi_w1[e].astype(BF16), w2)
            front = ((0, 0), (NSA_WINDOW, 0), (0, 0))
            o_a = _nsa_attention(q, ksel, vsel, jnp.pad(kwin, front), jnp.pad(vwin, front), cmp, gates, ovt,
                                 tsel, twin, tk)
            u = _conformer(glu, conv_dw_w[e], conv_dw_b[e][None, :], conv_ln_g[e][None, :],
                           conv_ln_b[e][None, :], tm)
            x = _even_out(x, o_a, u, ev_w_out[e].astype(BF16), tm)
        else:
            o = layer // 2
            lam_init = 0.8 - 0.6 * math.exp(-0.3 * layer)
            cq, ck, cv, dq, dk, dv = _odd_proj(x, g_mix, od_w_in[o].astype(BF16), tm)
            o_c = _diff_attention(diff_lambda[o], cq, ck, cv, tdiff, diff_norm_g[o][None, :], lam_init, tk)
            outs, lses = zip(*[_dilated_group(dq, dk, dv, tdil[n], dil) for n, (_, dil) in enumerate(DIL_PAIRS)])
            x = _odd_out(x, o_c, outs, lses, od_w_out[o].astype(BF16), tm)
        x = _ffn(x, ffn_norm_g[layer][None, :], ffn_w_gate[layer].astype(BF16), ffn_w_up[layer].astype(BF16),
                 ffn_conv_w[layer], ffn_conv_b[layer][None, :], ffn_w_down[layer].astype(BF16),
                 final_norm_g[None, :], tm, final=(layer == depth - 1))
    return x
```

```python
import functools
import math

import numpy as np
import jax
import jax.numpy as jnp
from jax import lax
from jax.experimental import pallas as pl
from jax.experimental.pallas import tpu as pltpu

F32, BF16 = jnp.float32, jnp.bfloat16

D_MODEL = 1024
HEAD_DIM = 64
HALF = 512
NSA_GROUPS = 2
NSA_HPG = 4
NSA_CMP_LEN = 32
NSA_CMP_STRIDE = 16
NSA_SEL_LEN = 64
NSA_TOP_N = 16
NSA_WINDOW = 512
NSA_FORCE = 1e6
CONV_WIDTH = 31
DIFF_HEADS = 4
DIL_HEADS = 8
DIL_PAIRS = ((128, 1), (512, 4), (2048, 16))
D_FF = 2816
REL_BUCKETS = 32
REL_MAX_DIST = 2048
REL_COL_NSA = 0
REL_COL_DIFF = 8
REL_COL_DIL = 12
EPS = 1e-6
NEG = -1e30

LANE = 128
TILE = 128
FF_CHUNK = 256
VMEM_LIMIT = 56 * 1024 * 1024

_Q0, _KVC0, _KSEL0, _VSEL0, _KWIN0, _VWIN0, _GATE0, _GLUA0, _GLUG0, _EV_COLS = (
    0, 1024, 1280, 1536, 1792, 2048, 2304, 2560, 3072, 3584)


def _cparams(n_axes):
    return pltpu.CompilerParams(dimension_semantics=("arbitrary",) * n_axes,
                                vmem_limit_bytes=VMEM_LIMIT)


def _resident(shape, index_map):
    return pl.BlockSpec(shape, index_map, pipeline_mode=pl.Buffered(1))


def _nt(a, b):
    return lax.dot_general(a, b, (((1,), (1,)), ((), ())), preferred_element_type=F32)


def _rms_bf16(x, g):
    ms = jnp.mean(x * x, axis=-1, keepdims=True)
    return (x * lax.rsqrt(ms + EPS) * g).astype(BF16)


def _rel_bucket(dist):
    n = jnp.maximum(dist, 0)
    max_exact = REL_BUCKETS // 2
    nf = jnp.maximum(n, 1).astype(F32)
    large = max_exact + (jnp.log(nf / max_exact) / math.log(REL_MAX_DIST / max_exact)
                         * (REL_BUCKETS - max_exact)).astype(jnp.int32)
    large = jnp.minimum(large, REL_BUCKETS - 1)
    return jnp.where(n < max_exact, n, large)


def _bias_tiles(table, col0, n_heads, dist, valid):
    bucket = _rel_bucket(jnp.asarray(dist, jnp.int32))
    onehot = (bucket[..., None] == jnp.arange(REL_BUCKETS, dtype=jnp.int32)).astype(F32)
    vals = jnp.einsum("tijb,bh->htij", onehot, table[:, col0:col0 + n_heads].astype(F32),
                      precision=lax.Precision.HIGHEST)
    return jnp.where(jnp.asarray(valid)[None], vals, NEG)


def _causal_tiles(table, col0, n_heads, n_q):
    i = np.arange(TILE)[:, None] - np.arange(TILE)[None, :]
    dist = np.stack([np.full((TILE, TILE), -1)] + [TILE * d + i for d in range(n_q)])
    return _bias_tiles(table, col0, n_heads, dist, dist >= 0)


def _window_tiles(table, col0, n_heads):
    i = np.arange(TILE)[:, None] - np.arange(TILE)[None, :]
    dist = np.stack([NSA_WINDOW - TILE * b + i for b in range(NSA_WINDOW // TILE + 1)])
    return _bias_tiles(table, col0, n_heads, dist, (dist >= 0) & (dist < NSA_WINDOW))


def _dilated_tiles(table, col0, n_heads, dil):
    i = np.arange(TILE)[:, None] - np.arange(TILE)[None, :]
    sub = np.stack([TILE + i, i])
    return _bias_tiles(table, col0, n_heads, sub * dil, (sub >= 0) & (sub <= TILE))


def _even_proj_kernel(x_ref, g_ref, w_ref, q_ref, kvc_ref, ksel_ref, vsel_ref, kwin_ref, vwin_ref,
                      gate_ref, glu_ref, *, tm):
    h = _rms_bf16(x_ref[0], g_ref[...])

    def mm(a, b):
        return jnp.dot(h, w_ref[:, a:b], preferred_element_type=F32)

    q_ref[0] = (mm(_Q0, _KVC0) * (HEAD_DIM ** -0.5)).astype(BF16)
    kvc_ref[0] = mm(_KVC0, _KSEL0)
    ks = mm(_KSEL0, _VSEL0)
    lane = lax.broadcasted_iota(jnp.int32, ks.shape, 1) % LANE
    pos = pl.program_id(1) * tm + lax.broadcasted_iota(jnp.int32, ks.shape, 0)
    onehot = jnp.where(lane - HEAD_DIM == pos // NSA_SEL_LEN, 1.0, 0.0)
    ksel_ref[0] = jnp.where(lane >= HEAD_DIM, onehot, ks).astype(BF16)
    vsel_ref[0] = mm(_VSEL0, _KWIN0).astype(BF16)
    kwin_ref[0] = mm(_KWIN0, _VWIN0).astype(BF16)
    vwin_ref[0] = mm(_VWIN0, _GATE0).astype(BF16)
    gate_ref[0] = jax.nn.sigmoid(mm(_GATE0, _GLUA0))
    glu_ref[0] = mm(_GLUA0, _GLUG0) * jax.nn.sigmoid(mm(_GLUG0, _EV_COLS))


def _even_proj(x, g, w, tm):
    B, S, D = x.shape
    row = lambda n: pl.BlockSpec((1, tm, n), lambda b, i: (b, i, 0))
    outs = [(1024, BF16), (256, F32), (256, BF16), (256, BF16), (256, BF16), (256, BF16), (256, F32), (512, F32)]
    return pl.pallas_call(
        functools.partial(_even_proj_kernel, tm=tm),
        grid=(B, S // tm),
        in_specs=[row(D), _resident((1, D), lambda b, i: (0, 0)), _resident((D, _EV_COLS), lambda b, i: (0, 0))],
        out_specs=[row(n) for n, _ in outs],
        out_shape=[jax.ShapeDtypeStruct((B, S, n), dt) for n, dt in outs],
        compiler_params=_cparams(2),
        name="even_proj",
    )(x, g, w)


def _even_proj_weights(w_in):
    D = w_in.shape[0]

    def slots(cols, n, width):
        p = cols.reshape(D, n, width)
        return jnp.pad(p, ((0, 0), (0, 0), (0, LANE - width))).reshape(D, n * LANE)

    kv0 = HALF
    parts = [slots(w_in[:, :HALF], 8, HEAD_DIM),
             w_in[:, kv0:kv0 + 256]]
    for br in (1, 2):
        for e in (0, 1):
            c0 = kv0 + br * 256 + e * 128
            parts.append(slots(w_in[:, c0:c0 + 128], 2, HEAD_DIM))
    g0 = kv0 + 768
    parts.append(slots(w_in[:, g0:g0 + 24], 2, 12))
    parts.append(w_in[:, g0 + 24:])
    return jnp.concatenate(parts, axis=1).astype(BF16)


def _compress_kernel(r_ref, pos_ref, w1_ref, w2_ref, o_ref):
    r = r_ref[0, 0]
    nc = r.shape[0]
    half = NSA_CMP_STRIDE * HEAD_DIM
    pos = pos_ref[0]
    lo = (r + pos[:, :half]).astype(BF16)
    hi = (pltpu.roll(r, nc - 1, 0) + pos[:, half:]).astype(BF16)
    hid = (jnp.dot(lo, w1_ref[0, :half, :], preferred_element_type=F32)
           + jnp.dot(hi, w1_ref[0, half:, :], preferred_element_type=F32))
    hid = hid * jax.nn.sigmoid(hid)
    o_ref[0, 0] = jnp.dot(hid.astype(BF16), w2_ref[0], preferred_element_type=F32).astype(BF16)


def _compress(r, pos, w1, w2):
    B, _, NC, W = r.shape
    return pl.pallas_call(
        _compress_kernel,
        grid=(B, 4),
        in_specs=[pl.BlockSpec((1, 1, NC, W), lambda b, c: (b, c, 0, 0)),
                  pl.BlockSpec((1, 1, 2 * W), lambda b, c: (c // 2, 0, 0)),
                  pl.BlockSpec((1, 2 * W, 256), lambda b, c: (c // 2, 0, 0)),
                  pl.BlockSpec((1, 256, LANE), lambda b, c: (c // 2, 0, 0))],
        out_specs=pl.BlockSpec((1, 1, NC, LANE), lambda b, c: (b, c, 0, 0)),
        out_shape=jax.ShapeDtypeStruct((B, 4, NC, LANE), BF16),
        compiler_params=_cparams(2),
        name="nsa_compress",
    )(r, pos, w1, w2)


def _stack_heads(x, n):
    return jnp.concatenate([x] * n, axis=0)


def _nsa_kernel(q_ref, ksel_ref, vsel_ref, kwin_ref, vwin_ref, kcmp_ref, vcmp_ref, gate_ref, ovt_ref,
                tsel_ref, twin_ref, o_ref, acc_ref, *, tk, n_cmp, n_top):
    qi = pl.program_id(2)
    s0 = qi * TILE
    H = NSA_HPG
    qs = jnp.concatenate([q_ref[0, :, LANE * h:LANE * (h + 1)] for h in range(H)], axis=0)

    nc = kcmp_ref.shape[2]
    s_c = _nt(qs, kcmp_ref[0, 0])
    t_row = s0 + lax.broadcasted_iota(jnp.int32, (TILE, nc), 0)
    n_idx = lax.broadcasted_iota(jnp.int32, (TILE, nc), 1)
    ok = jnp.where(n_idx * NSA_CMP_STRIDE + (NSA_CMP_LEN - 1) <= t_row, 1.0, 0.0)
    ok = _stack_heads(jnp.where(n_idx < n_cmp, ok, 0.0), H)
    s_c = jnp.where(ok > 0, s_c, NEG)
    p_c = jnp.exp(s_c - jnp.max(s_c, axis=-1, keepdims=True)) * ok
    l_c = jnp.sum(p_c, axis=-1, keepdims=True)
    p_c = p_c * jnp.where(l_c > 0, 1.0 / l_c, 0.0)
    o_c = jnp.dot(p_c.astype(BF16), vcmp_ref[0, 0], preferred_element_type=F32)

    p_cat = jnp.concatenate([p_c[TILE * h:TILE * (h + 1)] for h in range(H)], axis=1).astype(BF16)
    imp = _nt(ovt_ref[...], p_cat)[HEAD_DIM:]
    nj = imp.shape[0]
    j = lax.broadcasted_iota(jnp.int32, (nj, TILE), 0)
    t = s0 + lax.broadcasted_iota(jnp.int32, (nj, TILE), 1)
    cur = t // NSA_SEL_LEN
    forced = jnp.where(j == 0, 1.0, 0.0) + jnp.where(j == cur, 1.0, 0.0) + jnp.where(j == cur - 1, 1.0, 0.0)
    imp = jnp.where(forced > 0, NSA_FORCE, imp)
    imp = jnp.where(j * NSA_SEL_LEN <= t, imp, NEG)
    rank = jnp.zeros((nj, TILE), F32)
    for k in range(nj):
        vk = imp[k:k + 1, :]
        rank = rank + jnp.where(j > k, jnp.where(vk >= imp, 1.0, 0.0), jnp.where(vk > imp, 1.0, 0.0))
    mneg = jnp.where(rank < n_top, 0.0, NEG)
    m_t = jnp.concatenate([jnp.zeros((LANE - nj, TILE), F32), mneg], axis=0)
    q_aug = qs + _stack_heads(m_t.T.astype(BF16), H)

    sub = tk // TILE

    def body(kc, carry):
        m, l = carry
        k0 = pl.multiple_of(kc * tk, tk)
        s = _nt(q_aug, ksel_ref[0, pl.ds(k0, tk), :])
        rows = []
        for h in range(H):
            cols = [tsel_ref[h, jnp.maximum(qi - (kc * sub + c) + 1, 0)] for c in range(sub)]
            rows.append(jnp.concatenate(cols, axis=1))
        s = s + jnp.concatenate(rows, axis=0)
        m_new = jnp.maximum(m, jnp.max(s, axis=-1, keepdims=True))
        alpha = jnp.exp(m - m_new)
        p = jnp.exp(s - m_new)
        l = alpha * l + jnp.sum(p, axis=-1, keepdims=True)
        acc_ref[...] = acc_ref[...] * alpha + jnp.dot(
            p.astype(BF16), vsel_ref[0, pl.ds(k0, tk), :], preferred_element_type=F32)
        return m_new, l

    acc_ref[...] = jnp.zeros_like(acc_ref)
    _, l_s = lax.fori_loop(0, qi // sub + 1, body,
                           (jnp.full((H * TILE, 1), NEG, F32), jnp.zeros((H * TILE, 1), F32)))
    o_s = acc_ref[...] / l_s

    wlen = NSA_WINDOW + TILE
    s_w = _nt(qs, kwin_ref[0, pl.ds(pl.multiple_of(s0, TILE), wlen), :])
    nb = wlen // TILE
    s_w = s_w + jnp.concatenate(
        [jnp.concatenate([twin_ref[h, b] for b in range(nb)], axis=1) for h in range(H)], axis=0)
    kpos = s0 - NSA_WINDOW + lax.broadcasted_iota(jnp.int32, (1, wlen), 1)
    s_w = s_w + jnp.where(kpos >= 0, 0.0, NEG)
    p_w = jnp.exp(s_w - jnp.max(s_w, axis=-1, keepdims=True))
    l_w = jnp.sum(p_w, axis=-1, keepdims=True)
    o_w = jnp.dot(p_w.astype(BF16), vwin_ref[0, pl.ds(pl.multiple_of(s0, TILE), wlen), :],
                  preferred_element_type=F32) / l_w

    gt = gate_ref[0]

    def gate(br):
        return jnp.concatenate([gt[:, 3 * h + br:3 * h + br + 1] for h in range(H)], axis=0)

    o = gate(0) * o_c + gate(1) * o_s + gate(2) * o_w
    o_ref[0] = jnp.concatenate([o[TILE * h:TILE * (h + 1), :HEAD_DIM] for h in range(H)], axis=1).astype(BF16)


def _nsa_attention(q, ksel, vsel, kwin, vwin, cmp, gates, ovt, tsel, twin, tk):
    B, S, _ = q.shape
    NC = cmp.shape[2]
    n_q = S // TILE
    G = NSA_GROUPS
    kv = lambda rows: pl.BlockSpec((1, rows, LANE), lambda g, b, i: (b, 0, g))
    return pl.pallas_call(
        functools.partial(_nsa_kernel, tk=tk, n_cmp=S // NSA_CMP_STRIDE - 1,
                          n_top=min(NSA_TOP_N, S // NSA_SEL_LEN)),
        grid=(G, B, n_q),
        in_specs=[pl.BlockSpec((1, TILE, NSA_HPG * LANE), lambda g, b, i: (b, i, g)),
                  kv(S), kv(S), kv(S + NSA_WINDOW), kv(S + NSA_WINDOW),
                  pl.BlockSpec((1, 1, NC, LANE), lambda g, b, i: (b, g, 0, 0)),
                  pl.BlockSpec((1, 1, NC, LANE), lambda g, b, i: (b, G + g, 0, 0)),
                  pl.BlockSpec((1, TILE, LANE), lambda g, b, i: (b, i, g)),
                  _resident(ovt.shape, lambda g, b, i: (0, 0)),
                  _resident((NSA_HPG,) + tsel.shape[1:], lambda g, b, i: (g, 0, 0, 0)),
                  _resident((NSA_HPG,) + twin.shape[1:], lambda g, b, i: (g, 0, 0, 0))],
        out_specs=pl.BlockSpec((1, TILE, NSA_HPG * HEAD_DIM), lambda g, b, i: (b, i, g)),
        out_shape=jax.ShapeDtypeStruct((B, S, HALF), BF16),
        scratch_shapes=[pltpu.VMEM((NSA_HPG * TILE, LANE), F32)],
        compiler_params=_cparams(3),
        name="nsa_attention",
    )(q, ksel, vsel, kwin, vwin, cmp, cmp, gates, ovt, tsel, twin)


def _overlap_t(S):
    n_cmp = (S - NSA_CMP_LEN) // NSA_CMP_STRIDE + 1
    n_sel = S // NSA_SEL_LEN
    cmp_idx = np.arange(n_cmp)[:, None] * NSA_CMP_STRIDE + np.arange(NSA_CMP_LEN)[None, :]
    ov = (cmp_idx[:, :, None] // NSA_SEL_LEN == np.arange(n_sel)[None, None, :]).mean(axis=1)
    out = np.zeros((LANE, S // NSA_CMP_STRIDE), np.float32)
    out[HEAD_DIM:HEAD_DIM + n_sel, :n_cmp] = ov.T
    return jnp.asarray(np.tile(out, (1, NSA_HPG)), BF16)


def _conformer_kernel(x_ref, w_ref, b_ref, g_ref, beta_ref, o_ref, buf, *, tm):
    halo = 32

    @pl.when(pl.program_id(1) == 0)
    def _():
        buf[0:halo] = jnp.zeros((halo, buf.shape[1]), F32)

    buf[halo:halo + tm] = x_ref[0]
    acc = jnp.zeros((tm, buf.shape[1]), F32) + b_ref[...]
    for j in range(CONV_WIDTH):
        acc = acc + w_ref[j:j + 1, :] * buf[pl.ds(halo - (CONV_WIDTH - 1) + j, tm), :]
    mu = jnp.mean(acc, axis=-1, keepdims=True)
    var = jnp.mean(jnp.square(acc - mu), axis=-1, keepdims=True)
    y = (acc - mu) * lax.rsqrt(var + EPS) * g_ref[...] + beta_ref[...]
    o_ref[0] = (y * jax.nn.sigmoid(y)).astype(BF16)
    buf[0:halo] = buf[tm:tm + halo]


def _conformer(glu, w, b, g, beta, tm):
    B, S, C = glu.shape
    vec = lambda r: pl.BlockSpec((r, C), lambda bb, i: (0, 0))
    return pl.pallas_call(
        functools.partial(_conformer_kernel, tm=tm),
        grid=(B, S // tm),
        in_specs=[pl.BlockSpec((1, tm, C), lambda bb, i: (bb, i, 0)), vec(CONV_WIDTH), vec(1), vec(1), vec(1)],
        out_specs=pl.BlockSpec((1, tm, C), lambda bb, i: (bb, i, 0)),
        out_shape=jax.ShapeDtypeStruct((B, S, C), BF16),
        scratch_shapes=[pltpu.VMEM((tm + 32, C), F32)],
        compiler_params=_cparams(2),
        name="conformer_conv",
    )(glu, w, b, g, beta)


def _even_out_kernel(x_ref, a_ref, u_ref, w_ref, o_ref):
    o_ref[0] = (x_ref[0] + jnp.dot(a_ref[0], w_ref[:HALF, :], preferred_element_type=F32)
                + jnp.dot(u_ref[0], w_ref[HALF:, :], preferred_element_type=F32))


def _even_out(x, a, u, w, tm):
    B, S, D = x.shape
    row = lambda n: pl.BlockSpec((1, tm, n), lambda b, i: (b, i, 0))
    return pl.pallas_call(
        _even_out_kernel, grid=(B, S // tm),
        in_specs=[row(D), row(HALF), row(HALF), _resident((D, D), lambda b, i: (0, 0))],
        out_specs=row(D), out_shape=jax.ShapeDtypeStruct((B, S, D), F32),
        compiler_params=_cparams(2), name="even_out",
    )(x, a, u, w)


def _odd_out_kernel(x_ref, c_ref, o1_ref, o2_ref, o3_ref, l1_ref, l2_ref, l3_ref, w_ref, o_ref):
    l1, l2, l3 = l1_ref[0], l2_ref[0], l3_ref[0]
    m = jnp.maximum(jnp.maximum(l1, l2), l3)
    e1, e2, e3 = jnp.exp(l1 - m), jnp.exp(l2 - m), jnp.exp(l3 - m)
    od = (e1 * o1_ref[0] + e2 * o2_ref[0] + e3 * o3_ref[0]) / (e1 + e2 + e3)
    o_ref[0] = (x_ref[0] + jnp.dot(c_ref[0], w_ref[:HALF, :], preferred_element_type=F32)
                + jnp.dot(od.astype(BF16), w_ref[HALF:, :], preferred_element_type=F32))


def _odd_out(x, oc, os_, ls, w, tm):
    B, S, D = x.shape
    row = lambda n: pl.BlockSpec((1, tm, n), lambda b, i: (b, i, 0))
    return pl.pallas_call(
        _odd_out_kernel, grid=(B, S // tm),
        in_specs=[row(D)] + [row(HALF)] * 7 + [_resident((D, D), lambda b, i: (0, 0))],
        out_specs=row(D), out_shape=jax.ShapeDtypeStruct((B, S, D), F32),
        compiler_params=_cparams(2), name="odd_out",
    )(x, oc, *os_, *ls, w)


def _odd_proj_kernel(x_ref, g_ref, w_ref, cq_ref, ck_ref, cv_ref, dq_ref, dk_ref, dv_ref):
    h = _rms_bf16(x_ref[0], g_ref[...])
    scale = HEAD_DIM ** -0.5
    for n, (ref, s) in enumerate(((cq_ref, scale), (ck_ref, 1.0), (cv_ref, 1.0),
                                  (dq_ref, scale), (dk_ref, 1.0), (dv_ref, 1.0))):
        y = jnp.dot(h, w_ref[:, HALF * n:HALF * (n + 1)], preferred_element_type=F32)
        ref[0] = (y * s).astype(BF16)


def _odd_proj(x, g, w, tm):
    B, S, D = x.shape
    row = lambda n: pl.BlockSpec((1, tm, n), lambda b, i: (b, i, 0))
    return pl.pallas_call(
        _odd_proj_kernel, grid=(B, S // tm),
        in_specs=[row(D), _resident((1, D), lambda b, i: (0, 0)), _resident((D, 6 * HALF), lambda b, i: (0, 0))],
        out_specs=[row(HALF)] * 6,
        out_shape=[jax.ShapeDtypeStruct((B, S, HALF), BF16)] * 6,
        compiler_params=_cparams(2), name="odd_proj",
    )(x, g, w)


def _diff_kernel(lp_ref, q_ref, k_ref, v_ref, t_ref, g_ref, o_ref, acc_ref, *, tk, lam_init):
    qi = pl.program_id(2)
    lp = lp_ref[...]
    lam = (jnp.exp(jnp.sum(lp[0:1] * lp[1:2], axis=-1, keepdims=True))
           - jnp.exp(jnp.sum(lp[2:3] * lp[3:4], axis=-1, keepdims=True)) + lam_init)
    q = q_ref[0]
    lane = lax.broadcasted_iota(jnp.int32, q.shape, 1)
    zero = jnp.zeros_like(q)
    qs = jnp.concatenate([jnp.where(lane < HEAD_DIM, q, zero), jnp.where(lane >= HEAD_DIM, q, zero)], axis=0)
    sub = tk // TILE

    def body(kc, carry):
        m, l = carry
        k0 = pl.multiple_of(kc * tk, tk)
        s = _nt(qs, k_ref[0, pl.ds(k0, tk), :])
        bt = jnp.concatenate([t_ref[0, jnp.maximum(qi - (kc * sub + c) + 1, 0)] for c in range(sub)], axis=1)
        s = s + _stack_heads(bt, 2)
        m_new = jnp.maximum(m, jnp.max(s, axis=-1, keepdims=True))
        alpha = jnp.exp(m - m_new)
        p = jnp.exp(s - m_new)
        l = alpha * l + jnp.sum(p, axis=-1, keepdims=True)
        acc_ref[...] = acc_ref[...] * alpha + jnp.dot(
            p.astype(BF16), v_ref[0, pl.ds(k0, tk), :], preferred_element_type=F32)
        return m_new, l

    acc_ref[...] = jnp.zeros_like(acc_ref)
    _, l = lax.fori_loop(0, qi // sub + 1, body,
                         (jnp.full((2 * TILE, 1), NEG, F32), jnp.zeros((2 * TILE, 1), F32)))
    on = acc_ref[...] / l
    o = on[:TILE] - lam * on[TILE:]
    o = o * lax.rsqrt(jnp.mean(o * o, axis=-1, keepdims=True) + EPS) * g_ref[...] * (1.0 - lam_init)
    o_ref[0] = o.astype(BF16)


def _diff_attention(lp, q, k, v, tiles, g, lam_init, tk):
    B, S, _ = q.shape
    n_q = S // TILE
    return pl.pallas_call(
        functools.partial(_diff_kernel, tk=tk, lam_init=lam_init),
        grid=(DIFF_HEADS, B, n_q),
        in_specs=[pl.BlockSpec(lp.shape, lambda h, b, i: (0, 0)),
                  pl.BlockSpec((1, TILE, LANE), lambda h, b, i: (b, i, h)),
                  pl.BlockSpec((1, S, LANE), lambda h, b, i: (b, 0, h)),
                  pl.BlockSpec((1, S, LANE), lambda h, b, i: (b, 0, h)),
                  _resident((1,) + tiles.shape[1:], lambda h, b, i: (h, 0, 0, 0)),
                  pl.BlockSpec(g.shape, lambda h, b, i: (0, 0))],
        out_specs=pl.BlockSpec((1, TILE, LANE), lambda h, b, i: (b, i, h)),
        out_shape=jax.ShapeDtypeStruct((B, S, HALF), BF16),
        scratch_shapes=[pltpu.VMEM((2 * TILE, LANE), F32)],
        compiler_params=_cparams(3), name="diff_attention",
    )(lp, q, k, v, tiles, g)


def _dilated_kernel(q_ref, k_ref, v_ref, t_ref, o_ref, lse_ref):
    n = pl.program_id(2)
    H = 4
    q = q_ref[0]
    lane = lax.broadcasted_iota(jnp.int32, q.shape, 1) // HEAD_DIM
    zero = jnp.zeros_like(q)
    qs = jnp.concatenate([jnp.where(lane == h, q, zero) for h in range(H)], axis=0)
    cur = pl.multiple_of(n * TILE, TILE)
    prev = pl.multiple_of(jnp.maximum(n - 1, 0) * TILE, TILE)
    kcat = jnp.concatenate([k_ref[0, pl.ds(prev, TILE), :], k_ref[0, pl.ds(cur, TILE), :]], axis=0)
    vcat = jnp.concatenate([v_ref[0, pl.ds(prev, TILE), :], v_ref[0, pl.ds(cur, TILE), :]], axis=0)
    s = _nt(qs, kcat)
    s = s + jnp.concatenate([jnp.concatenate([t_ref[h, 0], t_ref[h, 1]], axis=1) for h in range(H)], axis=0)
    col = lax.broadcasted_iota(jnp.int32, (1, 2 * TILE), 1)
    s = s + jnp.where((col < TILE) & (n == 0), NEG, 0.0)
    m = jnp.max(s, axis=-1, keepdims=True)
    p = jnp.exp(s - m)
    l = jnp.sum(p, axis=-1, keepdims=True)
    r = jnp.dot(p.astype(BF16), vcat, preferred_element_type=F32) / l
    lse = m + jnp.log(l)
    lane_f = lax.broadcasted_iota(jnp.int32, (TILE, q.shape[1]), 1) // HEAD_DIM
    out = jnp.zeros((TILE, q.shape[1]), F32)
    lse_out = jnp.zeros((TILE, q.shape[1]), F32)
    for h in range(H):
        out = jnp.where(lane_f == h, r[TILE * h:TILE * (h + 1)], out)
        lse_out = jnp.where(lane_f == h, lse[TILE * h:TILE * (h + 1)], lse_out)
    o_ref[0] = out.astype(BF16)
    lse_ref[0] = lse_out


def _dilated_group(q, k, v, tiles, dil):
    B, S, C = q.shape
    L = S // dil
    W = 4 * HEAD_DIM
    view = lambda a: a.reshape(B, L, dil * C)
    qb = pl.BlockSpec((1, TILE, W), lambda lb, b, n: (b, n, lb))
    kvb = pl.BlockSpec((1, L, W), lambda lb, b, n: (b, 0, lb))
    o, lse = pl.pallas_call(
        _dilated_kernel,
        grid=(dil * C // W, B, L // TILE),
        in_specs=[qb, kvb, kvb, _resident((4,) + tiles.shape[1:], lambda lb, b, n: (lb % 2, 0, 0, 0))],
        out_specs=[qb, qb],
        out_shape=[jax.ShapeDtypeStruct((B, L, dil * C), BF16), jax.ShapeDtypeStruct((B, L, dil * C), F32)],
        compiler_params=_cparams(3), name=f"dilated_attention_d{dil}",
    )(view(q), view(k), view(v), tiles)
    return o.reshape(B, S, C), lse.reshape(B, S, C)


def _ffn_kernel(x_ref, g_ref, wg_ref, wu_ref, cw_ref, cb_ref, wd_ref, fg_ref, o_ref, gbuf, halo, *, tm, final):
    x = x_ref[0]
    h = _rms_bf16(x, g_ref[...])

    @pl.when(pl.program_id(1) == 0)
    def _():
        halo[...] = jnp.zeros_like(halo)

    o_ref[0] = x
    for c in range(D_FF // FF_CHUNK):
        cs = slice(FF_CHUNK * c, FF_CHUNK * (c + 1))
        gg = jnp.dot(h, wg_ref[:, cs], preferred_element_type=F32)
        gbuf[0:8] = halo[c]
        gbuf[8:8 + tm] = gg
        halo[c] = gg[tm - 8:tm]
        conv = (cw_ref[0:1, cs] * gbuf[pl.ds(6, tm), :] + cw_ref[1:2, cs] * gbuf[pl.ds(7, tm), :]
                + cw_ref[2:3, cs] * gg + cb_ref[:, cs])
        act = 0.5 * conv * (1.0 + lax.erf(conv * (2.0 ** -0.5)))
        act = act * jnp.dot(h, wu_ref[:, cs], preferred_element_type=F32)
        o_ref[0] += jnp.dot(act.astype(BF16), wd_ref[cs, :], preferred_element_type=F32)
    if final:
        y = o_ref[0]
        o_ref[0] = y * lax.rsqrt(jnp.mean(y * y, axis=-1, keepdims=True) + EPS) * fg_ref[...]


def _ffn(x, g, wg, wu, cw, cb, wd, fg, tm, final):
    B, S, D = x.shape
    row = pl.BlockSpec((1, tm, D), lambda b, i: (b, i, 0))
    const = lambda shape: _resident(shape, lambda b, i: (0,) * len(shape))
    return pl.pallas_call(
        functools.partial(_ffn_kernel, tm=tm, final=final),
        grid=(B, S // tm),
        in_specs=[row, const((1, D)), const((D, D_FF)), const((D, D_FF)), const(cw.shape), const((1, D_FF)),
                  const((D_FF, D)), const((1, D))],
        out_specs=row, out_shape=jax.ShapeDtypeStruct((B, S, D), F32),
        scratch_shapes=[pltpu.VMEM((tm + 8, FF_CHUNK), F32), pltpu.VMEM((D_FF // FF_CHUNK, 8, FF_CHUNK), F32)],
        compiler_params=_cparams(2), name="conv_glu_ffn",
    )(x, g, wg, wu, cw, cb, wd, fg)


def kernel(x, rel_bias_table, mix_norm_g, ffn_norm_g, ffn_w_gate, ffn_w_up, ffn_conv_w, ffn_conv_b, ffn_w_down,
           ev_w_in, ev_w_out, nsa_phi_pos, nsa_phi_w1, nsa_phi_w2, conv_dw_w, conv_dw_b, conv_ln_g, conv_ln_b,
           od_w_in, od_w_out, diff_lambda, diff_norm_g, final_norm_g):
    B, S, D = x.shape
    depth = mix_norm_g.shape[0]
    assert D == D_MODEL and S % 2048 == 0 and S // NSA_SEL_LEN <= HEAD_DIM
    tm = 512
    tk = 256
    n_q = S // TILE
    NC = S // NSA_CMP_STRIDE

    tsel = _causal_tiles(rel_bias_table, REL_COL_NSA, NSA_GROUPS * NSA_HPG, n_q)
    twin = _window_tiles(rel_bias_table, REL_COL_NSA, NSA_GROUPS * NSA_HPG)
    tdiff = _causal_tiles(rel_bias_table, REL_COL_DIFF, DIFF_HEADS, n_q)
    tdil = [_dilated_tiles(rel_bias_table, REL_COL_DIL, DIL_HEADS, dil) for _, dil in DIL_PAIRS]
    ovt = _overlap_t(S)

    for layer in range(depth):
        g_mix = mix_norm_g[layer][None, :]
        if layer % 2 == 0:
            e = layer // 2
            q, kvc, ksel, vsel, kwin, vwin, gates, glu = _even_proj(x, g_mix, _even_proj_weights(ev_w_in[e]), tm)
            r = kvc.reshape(B, NC, NSA_CMP_STRIDE, 4, HEAD_DIM).transpose(0, 3, 1, 2, 4)
            r = r.reshape(B, 4, NC, NSA_CMP_STRIDE * HEAD_DIM)
            w2 = jnp.pad(nsa_phi_w2[e], ((0, 0), (0, 0), (0, LANE - HEAD_DIM))).astype(BF16)
            cmp = _compress(r, nsa_phi_pos[e].reshape(2, 1, NSA_CMP_LEN * HEAD_DIM), nsa_phi_w1[e].astype(BF16), w2)
            front = ((0, 0), (NSA_WINDOW, 0), (0, 0))
            o_a = _nsa_attention(q, ksel, vsel, jnp.pad(kwin, front), jnp.pad(vwin, front), cmp, gates, ovt,
                                 tsel, twin, tk)
            u = _conformer(glu, conv_dw_w[e], conv_dw_b[e][None, :], conv_ln_g[e][None, :],
                           conv_ln_b[e][None, :], tm)
            x = _even_out(x, o_a, u, ev_w_out[e].astype(BF16), tm)
        else:
            o = layer // 2
            lam_init = 0.8 - 0.6 * math.exp(-0.3 * layer)
            cq, ck, cv, dq, dk, dv = _odd_proj(x, g_mix, od_w_in[o].astype(BF16), tm)
            o_c = _diff_attention(diff_lambda[o], cq, ck, cv, tdiff, diff_norm_g[o][None, :], lam_init, tk)
            outs, lses = zip(*[_dilated_group(dq, dk, dv, tdil[n], dil) for n, (_, dil) in enumerate(DIL_PAIRS)])
            x = _odd_out(x, o_c, outs, lses, od_w_out[o].astype(BF16), tm)
        x = _ffn(x, ffn_norm_g[layer][None, :], ffn_w_gate[layer].astype(BF16), ffn_w_up[layer].astype(BF16),
                 ffn_conv_w[layer], ffn_conv_b[layer][None, :], ffn_w_down[layer].astype(BF16),
                 final_norm_g[None, :], tm, final=(layer == depth - 1))
    return x
```

```python
import functools
import math

import numpy as np
import jax
import jax.numpy as jnp
from jax import lax
from jax.experimental import pallas as pl
from jax.experimental.pallas import tpu as pltpu

F32, BF16 = jnp.float32, jnp.bfloat16

D_MODEL = 1024
HEAD_DIM = 64
HALF = 512
NSA_GROUPS = 2
NSA_HPG = 4
NSA_CMP_LEN = 32
NSA_CMP_STRIDE = 16
NSA_SEL_LEN = 64
NSA_TOP_N = 16
NSA_WINDOW = 512
NSA_FORCE = 1e6
CONV_WIDTH = 31
DIFF_HEADS = 4
DIL_HEADS = 8
DIL_PAIRS = ((128, 1), (512, 4), (2048, 16))
D_FF = 2816
REL_BUCKETS = 32
REL_MAX_DIST = 2048
REL_COL_NSA = 0
REL_COL_DIFF = 8
REL_COL_DIL = 12
EPS = 1e-6
NEG = -1e30

LANE = 128
TILE = 128
FF_CHUNK = 256
VMEM_LIMIT = 56 * 1024 * 1024

_Q0, _KVC0, _KSEL0, _VSEL0, _KWIN0, _VWIN0, _GATE0, _GLUA0, _GLUG0, _EV_COLS = (
    0, 1024, 1280, 1536, 1792, 2048, 2304, 2560, 3072, 3584)


def _cparams(n_axes):
    return pltpu.CompilerParams(dimension_semantics=("arbitrary",) * n_axes,
                                vmem_limit_bytes=VMEM_LIMIT)


def _resident(shape, index_map):
    return pl.BlockSpec(shape, index_map, pipeline_mode=pl.Buffered(1))


def _nt(a, b):
    return lax.dot_general(a, b, (((1,), (1,)), ((), ())), preferred_element_type=F32)


def _rms_bf16(x, g):
    ms = jnp.mean(x * x, axis=-1, keepdims=True)
    return (x * lax.rsqrt(ms + EPS) * g).astype(BF16)


def _rel_bucket(dist):
    n = jnp.maximum(dist, 0)
    max_exact = REL_BUCKETS // 2
    nf = jnp.maximum(n, 1).astype(F32)
    large = max_exact + (jnp.log(nf / max_exact) / math.log(REL_MAX_DIST / max_exact)
                         * (REL_BUCKETS - max_exact)).astype(jnp.int32)
    large = jnp.minimum(large, REL_BUCKETS - 1)
    return jnp.where(n < max_exact, n, large)


def _bias_tiles(table, col0, n_heads, dist, valid):
    bucket = _rel_bucket(jnp.asarray(dist, jnp.int32))
    onehot = (bucket[..., None] == jnp.arange(REL_BUCKETS, dtype=jnp.int32)).astype(F32)
    vals = jnp.einsum("tijb,bh->htij", onehot, table[:, col0:col0 + n_heads].astype(F32),
                      precision=lax.Precision.HIGHEST)
    return jnp.where(jnp.asarray(valid)[None], vals, NEG)


def _n_causal_tiles(n_q):
    half = REL_BUCKETS // 2
    n_far = math.ceil(half * (REL_MAX_DIST / half) ** ((half - 1) / half)) + 1
    d_far = (n_far + 2 * (TILE - 1)) // TILE + 1
    far = np.arange(TILE * d_far - (TILE - 1), TILE * (d_far + 2), dtype=np.float32)
    assert np.all(half + np.floor(np.log(far / half) / math.log(REL_MAX_DIST / half) * half) >= REL_BUCKETS - 1)
    return min(n_q, d_far + 1) + 1


def _causal_tiles(table, col0, n_heads, n_q, keys_on_rows=False):
    i = np.arange(TILE)[:, None] - np.arange(TILE)[None, :]
    i = -i if keys_on_rows else i
    dist = np.stack([np.full((TILE, TILE), -1)] + [TILE * d + i for d in range(_n_causal_tiles(n_q) - 1)])
    return _bias_tiles(table, col0, n_heads, dist, dist >= 0)


def _window_tiles(table, col0, n_heads):
    i = np.arange(TILE)[:, None] - np.arange(TILE)[None, :]
    dist = np.stack([NSA_WINDOW - TILE * b + i for b in range(NSA_WINDOW // TILE + 1)])
    return _bias_tiles(table, col0, n_heads, dist, (dist >= 0) & (dist < NSA_WINDOW))


def _dilated_tiles(table, col0, n_heads, dil):
    i = np.arange(TILE)[:, None] - np.arange(TILE)[None, :]
    sub = np.stack([TILE + i, i])
    return _bias_tiles(table, col0, n_heads, sub * dil, (sub >= 0) & (sub <= TILE))


def _even_proj_kernel(x_ref, g_ref, w_ref, q_ref, kvc_ref, ksel_ref, vsel_ref, kwin_ref, vwin_ref,
                      gate_ref, glu_ref, *, tm):
    h = _rms_bf16(x_ref[0], g_ref[...])

    def mm(a, b):
        return jnp.dot(h, w_ref[:, a:b], preferred_element_type=F32)

    q_ref[0] = (mm(_Q0, _KVC0) * (HEAD_DIM ** -0.5)).astype(BF16)
    kvc_ref[0] = mm(_KVC0, _KSEL0)
    ks = mm(_KSEL0, _VSEL0)
    lane = lax.broadcasted_iota(jnp.int32, ks.shape, 1) % LANE
    pos = pl.program_id(1) * tm + lax.broadcasted_iota(jnp.int32, ks.shape, 0)
    onehot = jnp.where(lane - HEAD_DIM == pos // NSA_SEL_LEN, 1.0, 0.0)
    ksel_ref[0] = jnp.where(lane >= HEAD_DIM, onehot, ks).astype(BF16)
    vsel_ref[0] = mm(_VSEL0, _KWIN0).astype(BF16)
    kwin_ref[0] = mm(_KWIN0, _VWIN0).astype(BF16)
    vwin_ref[0] = mm(_VWIN0, _GATE0).astype(BF16)
    gate_ref[0] = jax.nn.sigmoid(mm(_GATE0, _GLUA0))
    glu_ref[0] = mm(_GLUA0, _GLUG0) * jax.nn.sigmoid(mm(_GLUG0, _EV_COLS))


def _even_proj(x, g, w, tm):
    B, S, D = x.shape
    row = lambda n: pl.BlockSpec((1, tm, n), lambda b, i: (b, i, 0))
    outs = [(1024, BF16), (256, F32), (256, BF16), (256, BF16), (256, BF16), (256, BF16), (256, F32), (512, F32)]
    return pl.pallas_call(
        functools.partial(_even_proj_kernel, tm=tm),
        grid=(B, S // tm),
        in_specs=[row(D), _resident((1, D), lambda b, i: (0, 0)), _resident((D, _EV_COLS), lambda b, i: (0, 0))],
        out_specs=[row(n) for n, _ in outs],
        out_shape=[jax.ShapeDtypeStruct((B, S, n), dt) for n, dt in outs],
        compiler_params=_cparams(2),
        name="even_proj",
    )(x, g, w)


def _even_proj_weights(w_in):
    D = w_in.shape[0]

    def slots(cols, n, width):
        p = cols.reshape(D, n, width)
        return jnp.pad(p, ((0, 0), (0, 0), (0, LANE - width))).reshape(D, n * LANE)

    kv0 = HALF
    parts = [slots(w_in[:, :HALF], 8, HEAD_DIM),
             w_in[:, kv0:kv0 + 256]]
    for br in (1, 2):
        for e in (0, 1):
            c0 = kv0 + br * 256 + e * 128
            parts.append(slots(w_in[:, c0:c0 + 128], 2, HEAD_DIM))
    g0 = kv0 + 768
    parts.append(slots(w_in[:, g0:g0 + 24], 2, 12))
    parts.append(w_in[:, g0 + 24:])
    return jnp.concatenate(parts, axis=1).astype(BF16)


def _compress_kernel(r_ref, pos_ref, w1_ref, w2_ref, o_ref):
    r = r_ref[0, 0]
    nc = r.shape[0]
    half = NSA_CMP_STRIDE * HEAD_DIM
    pos = pos_ref[0]
    lo = (r + pos[:, :half]).astype(BF16)
    hi = (pltpu.roll(r, nc - 1, 0) + pos[:, half:]).astype(BF16)
    hid = (jnp.dot(lo, w1_ref[0, :half, :], preferred_element_type=F32)
           + jnp.dot(hi, w1_ref[0, half:, :], preferred_element_type=F32))
    hid = hid * jax.nn.sigmoid(hid)
    o_ref[0, 0] = jnp.dot(hid.astype(BF16), w2_ref[0], preferred_element_type=F32).astype(BF16)


def _compress(r, pos, w1, w2):
    B, _, NC, W = r.shape
    return pl.pallas_call(
        _compress_kernel,
        grid=(B, 4),
        in_specs=[pl.BlockSpec((1, 1, NC, W), lambda b, c: (b, c, 0, 0)),
                  pl.BlockSpec((1, 1, 2 * W), lambda b, c: (c // 2, 0, 0)),
                  pl.BlockSpec((1, 2 * W, 256), lambda b, c: (c // 2, 0, 0)),
                  pl.BlockSpec((1, 256, LANE), lambda b, c: (c // 2, 0, 0))],
        out_specs=pl.BlockSpec((1, 1, NC, LANE), lambda b, c: (b, c, 0, 0)),
        out_shape=jax.ShapeDtypeStruct((B, 4, NC, LANE), BF16),
        compiler_params=_cparams(2),
        name="nsa_compress",
    )(r, pos, w1, w2)


def _stack_heads(x, n):
    return jnp.concatenate([x] * n, axis=0)


def _nsa_kernel(q_ref, ksel_ref, vsel_ref, kwin_ref, vwin_ref, cmp_ref, gate_ref, ovt_ref,
                tsel_ref, twin_ref, o_ref, acc_ref, *, tk, n_cmp, n_top, nt):
    qi = pl.program_id(1)
    s0 = qi * TILE
    H, G = NSA_HPG, NSA_GROUPS
    nc = cmp_ref.shape[2]
    wlen = NSA_WINDOW + TILE
    nb = wlen // TILE
    w0 = pl.multiple_of(s0, TILE)

    t_row = s0 + lax.broadcasted_iota(jnp.int32, (TILE, nc), 0)
    n_idx = lax.broadcasted_iota(jnp.int32, (TILE, nc), 1)
    ok = jnp.where(n_idx * NSA_CMP_STRIDE + (NSA_CMP_LEN - 1) <= t_row, 1.0, 0.0)
    ok = _stack_heads(jnp.where(n_idx < n_cmp, ok, 0.0), H)
    kpos = s0 - NSA_WINDOW + lax.broadcasted_iota(jnp.int32, (1, wlen), 1)
    wmask = jnp.where(kpos >= 0, 0.0, NEG)

    q_aug, o_fix = [], []
    for g in range(G):
        qs = jnp.concatenate([q_ref[0, :, LANE * (H * g + h):LANE * (H * g + h + 1)] for h in range(H)], axis=0)
        s_c = jnp.where(ok > 0, _nt(qs, cmp_ref[0, g]), NEG)
        p_c = jnp.exp(s_c - jnp.max(s_c, axis=-1, keepdims=True)) * ok
        l_c = jnp.sum(p_c, axis=-1, keepdims=True)
        p_c = p_c * jnp.where(l_c > 0, 1.0 / l_c, 0.0)
        o_c = jnp.dot(p_c.astype(BF16), cmp_ref[0, G + g], preferred_element_type=F32)
        p_cat = jnp.concatenate([p_c[TILE * h:TILE * (h + 1)] for h in range(H)], axis=1).astype(BF16)
        imp = _nt(ovt_ref[...], p_cat)[HEAD_DIM:]
        nj = imp.shape[0]
        j = lax.broadcasted_iota(jnp.int32, (nj, TILE), 0)
        t = s0 + lax.broadcasted_iota(jnp.int32, (nj, TILE), 1)
        cur = t // NSA_SEL_LEN
        forced = jnp.where(j == 0, 1.0, 0.0) + jnp.where(j == cur, 1.0, 0.0) + jnp.where(j == cur - 1, 1.0, 0.0)
        imp = jnp.where(forced > 0, NSA_FORCE, imp)
        imp = jnp.where(j * NSA_SEL_LEN <= t, imp, NEG)
        rank = jnp.zeros((nj, TILE), F32)
        for k in range(nj):
            vk = imp[k:k + 1, :]
            rank = rank + jnp.where(j > k, jnp.where(vk >= imp, 1.0, 0.0), jnp.where(vk > imp, 1.0, 0.0))
        mneg = jnp.where(rank < n_top, 0.0, NEG)
        m_t = jnp.concatenate([jnp.zeros((LANE - nj, TILE), F32), mneg], axis=0)
        q_aug.append(qs + _stack_heads(m_t.T.astype(BF16), H))
        s_w = _nt(qs, kwin_ref[0, pl.ds(w0, wlen), LANE * g:LANE * (g + 1)])
        s_w = s_w + jnp.concatenate(
            [jnp.concatenate([twin_ref[H * g + h, b] for b in range(nb)], axis=1) for h in range(H)], axis=0)
        s_w = s_w + wmask
        p_w = jnp.exp(s_w - jnp.max(s_w, axis=-1, keepdims=True))
        l_w = jnp.sum(p_w, axis=-1, keepdims=True)
        o_w = jnp.dot(p_w.astype(BF16), vwin_ref[0, pl.ds(w0, wlen), LANE * g:LANE * (g + 1)],
                      preferred_element_type=F32) / l_w
        gt = gate_ref[0, :, LANE * g:LANE * (g + 1)]
        gate = lambda br: jnp.concatenate([gt[:, 3 * h + br:3 * h + br + 1] for h in range(H)], axis=0)
        o_fix.append((gate(0) * o_c + gate(2) * o_w, gate(1)))

    sub = tk // TILE

    def body(kc, carry):
        k0 = pl.multiple_of(kc * tk, tk)
        idx = [jnp.minimum(jnp.maximum(qi - (kc * sub + c) + 1, 0), nt - 1) for c in range(sub)]
        out = []
        for g in range(G):
            m, l = carry[2 * g], carry[2 * g + 1]
            s = _nt(q_aug[g], ksel_ref[0, pl.ds(k0, tk), LANE * g:LANE * (g + 1)])
            s = s + jnp.concatenate(
                [jnp.concatenate([tsel_ref[H * g + h, idx[c]] for c in range(sub)], axis=1) for h in range(H)], axis=0)
            m_new = jnp.maximum(m, jnp.max(s, axis=-1, keepdims=True))
            alpha = jnp.exp(m - m_new)
            p = jnp.exp(s - m_new)
            l = alpha * l + jnp.sum(p, axis=-1, keepdims=True)
            acc_ref[g] = acc_ref[g] * alpha + jnp.dot(
                p.astype(BF16), vsel_ref[0, pl.ds(k0, tk), LANE * g:LANE * (g + 1)], preferred_element_type=F32)
            out += [m_new, l]
        return tuple(out)

    acc_ref[...] = jnp.zeros_like(acc_ref)
    init = (jnp.full((H * TILE, 1), NEG, F32), jnp.zeros((H * TILE, 1), F32)) * G
    res = lax.fori_loop(0, qi // sub + 1, body, init)
    outs = []
    for g in range(G):
        o = o_fix[g][0] + o_fix[g][1] * (acc_ref[g] / res[2 * g + 1])
        outs += [o[TILE * h:TILE * (h + 1), :HEAD_DIM] for h in range(H)]
    o_ref[0] = jnp.concatenate(outs, axis=1).astype(BF16)


def _nsa_attention(q, ksel, vsel, kwin, vwin, cmp, gates, ovt, tsel, twin, tk):
    B, S, _ = q.shape
    n_q = S // TILE
    full = lambda a: pl.BlockSpec((1,) + a.shape[1:], lambda b, i: (b,) + (0,) * (a.ndim - 1))
    return pl.pallas_call(
        functools.partial(_nsa_kernel, tk=tk, n_cmp=S // NSA_CMP_STRIDE - 1,
                          n_top=min(NSA_TOP_N, S // NSA_SEL_LEN), nt=tsel.shape[1]),
        grid=(B, n_q),
        in_specs=[pl.BlockSpec((1, TILE, q.shape[2]), lambda b, i: (b, i, 0)),
                  full(ksel), full(vsel), full(kwin), full(vwin), full(cmp),
                  pl.BlockSpec((1, TILE, gates.shape[2]), lambda b, i: (b, i, 0)),
                  _resident(ovt.shape, lambda b, i: (0, 0)),
                  _resident(tsel.shape, lambda b, i: (0, 0, 0, 0)),
                  _resident(twin.shape, lambda b, i: (0, 0, 0, 0))],
        out_specs=pl.BlockSpec((1, TILE, HALF), lambda b, i: (b, i, 0)),
        out_shape=jax.ShapeDtypeStruct((B, S, HALF), BF16),
        scratch_shapes=[pltpu.VMEM((NSA_GROUPS, NSA_HPG * TILE, LANE), F32)],
        compiler_params=_cparams(2),
        name="nsa_attention",
    )(q, ksel, vsel, kwin, vwin, cmp, gates, ovt, tsel, twin)


def _overlap_t(S):
    n_cmp = (S - NSA_CMP_LEN) // NSA_CMP_STRIDE + 1
    n_sel = S // NSA_SEL_LEN
    cmp_idx = np.arange(n_cmp)[:, None] * NSA_CMP_STRIDE + np.arange(NSA_CMP_LEN)[None, :]
    ov = (cmp_idx[:, :, None] // NSA_SEL_LEN == np.arange(n_sel)[None, None, :]).mean(axis=1)
    out = np.zeros((LANE, S // NSA_CMP_STRIDE), np.float32)
    out[HEAD_DIM:HEAD_DIM + n_sel, :n_cmp] = ov.T
    return jnp.asarray(np.tile(out, (1, NSA_HPG)), BF16)


def _conformer_kernel(x_ref, w_ref, b_ref, g_ref, beta_ref, o_ref, buf, *, tm):
    halo = 32

    @pl.when(pl.program_id(1) == 0)
    def _():
        buf[0:halo] = jnp.zeros((halo, buf.shape[1]), F32)

    buf[halo:halo + tm] = x_ref[0]
    acc = jnp.zeros((tm, buf.shape[1]), F32) + b_ref[...]
    for j in range(CONV_WIDTH):
        acc = acc + w_ref[j:j + 1, :] * buf[pl.ds(halo - (CONV_WIDTH - 1) + j, tm), :]
    mu = jnp.mean(acc, axis=-1, keepdims=True)
    var = jnp.mean(jnp.square(acc - mu), axis=-1, keepdims=True)
    y = (acc - mu) * lax.rsqrt(var + EPS) * g_ref[...] + beta_ref[...]
    o_ref[0] = (y * jax.nn.sigmoid(y)).astype(BF16)
    buf[0:halo] = buf[tm:tm + halo]


def _conformer(glu, w, b, g, beta, tm):
    B, S, C = glu.shape
    vec = lambda r: pl.BlockSpec((r, C), lambda bb, i: (0, 0))
    return pl.pallas_call(
        functools.partial(_conformer_kernel, tm=tm),
        grid=(B, S // tm),
        in_specs=[pl.BlockSpec((1, tm, C), lambda bb, i: (bb, i, 0)), vec(CONV_WIDTH), vec(1), vec(1), vec(1)],
        out_specs=pl.BlockSpec((1, tm, C), lambda bb, i: (bb, i, 0)),
        out_shape=jax.ShapeDtypeStruct((B, S, C), BF16),
        scratch_shapes=[pltpu.VMEM((tm + 32, C), F32)],
        compiler_params=_cparams(2),
        name="conformer_conv",
    )(glu, w, b, g, beta)


def _even_out_kernel(x_ref, a_ref, u_ref, w_ref, o_ref):
    o_ref[0] = (x_ref[0] + jnp.dot(a_ref[0], w_ref[:HALF, :], preferred_element_type=F32)
                + jnp.dot(u_ref[0], w_ref[HALF:, :], preferred_element_type=F32))


def _even_out(x, a, u, w, tm):
    B, S, D = x.shape
    row = lambda n: pl.BlockSpec((1, tm, n), lambda b, i: (b, i, 0))
    return pl.pallas_call(
        _even_out_kernel, grid=(B, S // tm),
        in_specs=[row(D), row(HALF), row(HALF), _resident((D, D), lambda b, i: (0, 0))],
        out_specs=row(D), out_shape=jax.ShapeDtypeStruct((B, S, D), F32),
        compiler_params=_cparams(2), name="even_out",
    )(x, a, u, w)


def _odd_out_kernel(x_ref, c_ref, o1_ref, o4_ref, o16_ref, l1_ref, l4_ref, l16_ref, w_ref, o_ref,
                    b_o4, b_o16, b_l4, b_l16, *, tm):
    for src, dst, d in ((o4_ref, b_o4, 4), (l4_ref, b_l4, 4), (o16_ref, b_o16, 16), (l16_ref, b_l16, 16)):
        for r in range(d):
            blk = src[0, r].astype(F32)
            for c in range(HALF // LANE):
                dst[c, pl.ds(r, tm // d, stride=d), :] = blk[:, LANE * c:LANE * (c + 1)]
    full = lambda buf: jnp.concatenate([buf[c] for c in range(HALF // LANE)], axis=1)
    l1, l2, l3 = l1_ref[0, 0], full(b_l4), full(b_l16)
    m = jnp.maximum(jnp.maximum(l1, l2), l3)
    e1, e2, e3 = jnp.exp(l1 - m), jnp.exp(l2 - m), jnp.exp(l3 - m)
    od = (e1 * o1_ref[0, 0] + e2 * full(b_o4) + e3 * full(b_o16)) / (e1 + e2 + e3)
    o_ref[0] = (x_ref[0] + jnp.dot(c_ref[0], w_ref[:HALF, :], preferred_element_type=F32)
                + jnp.dot(od.astype(BF16), w_ref[HALF:, :], preferred_element_type=F32))


def _odd_out(x, oc, os_, ls, w, tm):
    B, S, D = x.shape
    row = lambda n: pl.BlockSpec((1, tm, n), lambda b, i: (b, i, 0))
    dil = lambda d: pl.BlockSpec((1, d, tm // d, HALF), lambda b, i: (b, 0, i, 0))
    return pl.pallas_call(
        functools.partial(_odd_out_kernel, tm=tm), grid=(B, S // tm),
        in_specs=[row(D), row(HALF), dil(1), dil(4), dil(16), dil(1), dil(4), dil(16),
                  _resident((D, D), lambda b, i: (0, 0))],
        out_specs=row(D), out_shape=jax.ShapeDtypeStruct((B, S, D), F32),
        scratch_shapes=[pltpu.VMEM((HALF // LANE, tm, LANE), F32)] * 4,
        compiler_params=_cparams(2), name="odd_out",
    )(x, oc, *os_, *ls, w)


def _odd_proj_kernel(x_ref, g_ref, w_ref, cq_ref, ck_ref, cvt_ref, q1_ref, k1_ref, v1_ref,
                     q4_ref, k4_ref, v4_ref, q16_ref, k16_ref, v16_ref, ybuf, tbuf, *, tm):
    h = _rms_bf16(x_ref[0], g_ref[...])
    scale = HEAD_DIM ** -0.5
    for n, (ref, s) in enumerate(((cq_ref, scale), (ck_ref, 1.0))):
        y = jnp.dot(h, w_ref[:, HALF * n:HALF * (n + 1)], preferred_element_type=F32)
        ref[0] = (y * s).astype(BF16)
    tbuf[...] = jnp.dot(h, w_ref[:, 2 * HALF:3 * HALF], preferred_element_type=F32)
    cvt_ref[0, 0] = tbuf[...].T.astype(BF16)
    for n, (r1, r4, r16, s) in enumerate(((q1_ref, q4_ref, q16_ref, scale), (k1_ref, k4_ref, k16_ref, 1.0),
                                          (v1_ref, v4_ref, v16_ref, 1.0))):
        y = jnp.dot(h, w_ref[:, HALF * (n + 3):HALF * (n + 4)], preferred_element_type=F32) * s
        r1[0, 0] = y.astype(BF16)
        for c in range(HALF // LANE):
            ybuf[c] = y[:, LANE * c:LANE * (c + 1)]
        for d, ref in ((4, r4), (16, r16)):
            for r in range(d):
                ref[0, r] = jnp.concatenate([ybuf[c, pl.ds(r, tm // d, stride=d), :] for c in range(HALF // LANE)],
                                            axis=1).astype(BF16)


def _odd_proj(x, g, w, tm):
    B, S, D = x.shape
    row = lambda n: pl.BlockSpec((1, tm, n), lambda b, i: (b, i, 0))
    dil = lambda d: pl.BlockSpec((1, d, tm // d, HALF), lambda b, i: (b, 0, i, 0))
    dshape = lambda d: jax.ShapeDtypeStruct((B, d, S // d, HALF), BF16)
    return pl.pallas_call(
        functools.partial(_odd_proj_kernel, tm=tm), grid=(B, S // tm),
        in_specs=[row(D), _resident((1, D), lambda b, i: (0, 0)), _resident((D, 6 * HALF), lambda b, i: (0, 0))],
        out_specs=[row(HALF)] * 2 + [pl.BlockSpec((1, 1, HALF, tm), lambda b, i: (b, i, 0, 0))]
        + [dil(1)] * 3 + [dil(4)] * 3 + [dil(16)] * 3,
        out_shape=[jax.ShapeDtypeStruct((B, S, HALF), BF16)] * 2 + [jax.ShapeDtypeStruct((B, S // tm, HALF, tm), BF16)]
        + [dshape(1)] * 3 + [dshape(4)] * 3 + [dshape(16)] * 3,
        scratch_shapes=[pltpu.VMEM((HALF // LANE, tm, LANE), F32), pltpu.VMEM((tm, HALF), F32)],
        compiler_params=_cparams(2), name="odd_proj",
    )(x, g, w)


def _diff_kernel(lp_ref, q_ref, k_ref, vt_ref, t_ref, g_ref, o_ref, acc_ref, s_ref, *, tk, lam_init, nt):
    qi = pl.program_id(1)
    H = DIFF_HEADS
    lp = lp_ref[...]
    lam = (jnp.exp(jnp.sum(lp[0:1] * lp[1:2], axis=-1, keepdims=True))
           - jnp.exp(jnp.sum(lp[2:3] * lp[3:4], axis=-1, keepdims=True)) + lam_init)
    lane = lax.broadcasted_iota(jnp.int32, (TILE, LANE), 1)
    qs = []
    for h in range(H):
        q = q_ref[0, :, LANE * h:LANE * (h + 1)]
        zero = jnp.zeros_like(q)
        qs.append(jnp.concatenate([jnp.where(lane < HEAD_DIM, q, zero), jnp.where(lane >= HEAD_DIM, q, zero)], axis=0))
    sub = tk // TILE
    n_kc = k_ref.shape[1] // tk

    def qk(h, kc):
        k0 = pl.multiple_of(kc * tk, tk)
        return _nt(k_ref[0, pl.ds(k0, tk), LANE * h:LANE * (h + 1)], qs[h])

    def body(kc, carry):
        idx = [jnp.minimum(jnp.maximum(qi - (kc * sub + c) + 1, 0), nt - 1) for c in range(sub)]
        out = []
        s_next = s_ref[...]
        for h in range(H):
            s = s_next
            s_next = qk(h + 1, kc) if h + 1 < H else qk(0, jnp.minimum(kc + 1, n_kc - 1))
            m, l = carry[2 * h], carry[2 * h + 1]
            bt = jnp.concatenate([t_ref[h, idx[c]] for c in range(sub)], axis=0)
            s = s + jnp.concatenate([bt, bt], axis=1)
            m_new = jnp.maximum(m, jnp.max(s, axis=0, keepdims=True))
            alpha = jnp.exp(m - m_new)
            p = jnp.exp(s - m_new)
            l = alpha * l + jnp.sum(p, axis=0, keepdims=True)
            acc_ref[h] = acc_ref[h] * alpha + jnp.dot(
                vt_ref[0, kc, LANE * h:LANE * (h + 1), :], p.astype(BF16), preferred_element_type=F32)
            out += [m_new, l]
        s_ref[...] = s_next
        return tuple(out)

    s_ref[...] = qk(0, 0)
    acc_ref[...] = jnp.zeros_like(acc_ref)
    init = (jnp.full((1, 2 * TILE), NEG, F32), jnp.zeros((1, 2 * TILE), F32)) * H
    res = lax.fori_loop(0, qi // sub + 1, body, init)
    outs = []
    for h in range(H):
        on = acc_ref[h] / res[2 * h + 1]
        o = (on[:, :TILE] - lam * on[:, TILE:]).T
        o = o * lax.rsqrt(jnp.mean(o * o, axis=-1, keepdims=True) + EPS) * g_ref[...] * (1.0 - lam_init)
        outs.append(o.astype(BF16))
    o_ref[0] = jnp.concatenate(outs, axis=1)


def _diff_attention(lp, q, k, vt, tiles, g, lam_init, tk):
    B, S, _ = q.shape
    n_q = S // TILE
    return pl.pallas_call(
        functools.partial(_diff_kernel, tk=tk, lam_init=lam_init, nt=tiles.shape[1]),
        grid=(B, n_q),
        in_specs=[pl.BlockSpec(lp.shape, lambda b, i: (0, 0)),
                  pl.BlockSpec((1, TILE, HALF), lambda b, i: (b, i, 0)),
                  pl.BlockSpec((1, S, HALF), lambda b, i: (b, 0, 0)),
                  pl.BlockSpec((1, S // tk, HALF, tk), lambda b, i: (b, 0, 0, 0)),
                  _resident(tiles.shape, lambda b, i: (0, 0, 0, 0)),
                  pl.BlockSpec(g.shape, lambda b, i: (0, 0))],
        out_specs=pl.BlockSpec((1, TILE, HALF), lambda b, i: (b, i, 0)),
        out_shape=jax.ShapeDtypeStruct((B, S, HALF), BF16),
        scratch_shapes=[pltpu.VMEM((DIFF_HEADS, LANE, 2 * TILE), F32), pltpu.VMEM((tk, 2 * TILE), F32)],
        compiler_params=_cparams(2), name="diff_attention",
    )(lp, q, k, vt, tiles, g)


def _dilated_kernel(q_ref, k_ref, v_ref, t_ref, o_ref, lse_ref, *, nqb):
    n0 = pl.program_id(2) * nqb
    H = 4
    W = H * HEAD_DIM
    lane = lax.broadcasted_iota(jnp.int32, (TILE, W), 1) // HEAD_DIM
    col = lax.broadcasted_iota(jnp.int32, (1, 2 * TILE), 1)
    for qb in range(nqb):
        n = n0 + qb
        cur = pl.multiple_of(n * TILE, TILE)
        prev = pl.multiple_of(jnp.maximum(n - 1, 0) * TILE, TILE)
        pmask = jnp.where((col < TILE) & (n == 0), NEG, 0.0)
        outs, lses = [], []
        for c in range(2):
            cs = slice(W * c, W * (c + 1))
            q = q_ref[0, 0, pl.ds(qb * TILE, TILE), cs]
            zero = jnp.zeros_like(q)
            qs = jnp.concatenate([jnp.where(lane == h, q, zero) for h in range(H)], axis=0)
            kcat = jnp.concatenate([k_ref[0, 0, pl.ds(prev, TILE), cs], k_ref[0, 0, pl.ds(cur, TILE), cs]], axis=0)
            vcat = jnp.concatenate([v_ref[0, 0, pl.ds(prev, TILE), cs], v_ref[0, 0, pl.ds(cur, TILE), cs]], axis=0)
            s = _nt(qs, kcat)
            s = s + jnp.concatenate(
                [jnp.concatenate([t_ref[H * c + h, 0], t_ref[H * c + h, 1]], axis=1) for h in range(H)], axis=0)
            s = s + pmask
            m = jnp.max(s, axis=-1, keepdims=True)
            p = jnp.exp(s - m)
            l = jnp.sum(p, axis=-1, keepdims=True)
            r = jnp.dot(p.astype(BF16), vcat, preferred_element_type=F32) / l
            lse = m + jnp.log(l)
            out = jnp.zeros((TILE, W), F32)
            lse_out = jnp.zeros((TILE, W), F32)
            for h in range(H):
                out = jnp.where(lane == h, r[TILE * h:TILE * (h + 1)], out)
                lse_out = jnp.where(lane == h, lse[TILE * h:TILE * (h + 1)], lse_out)
            outs.append(out.astype(BF16))
            lses.append(lse_out)
        o_ref[0, 0, pl.ds(qb * TILE, TILE), :] = jnp.concatenate(outs, axis=1)
        lse_ref[0, 0, pl.ds(qb * TILE, TILE), :] = jnp.concatenate(lses, axis=1)


def _dilated_group(q, k, v, tiles, nqb):
    B, dil, L, C = q.shape
    nqb = min(nqb, L // TILE)
    qb = pl.BlockSpec((1, 1, nqb * TILE, C), lambda b, r, n: (b, r, n, 0))
    kvb = pl.BlockSpec((1, 1, L, C), lambda b, r, n: (b, r, 0, 0))
    return pl.pallas_call(
        functools.partial(_dilated_kernel, nqb=nqb),
        grid=(B, dil, L // (nqb * TILE)),
        in_specs=[qb, kvb, kvb, _resident(tiles.shape, lambda b, r, n: (0, 0, 0, 0))],
        out_specs=[qb, qb],
        out_shape=[jax.ShapeDtypeStruct((B, dil, L, C), BF16), jax.ShapeDtypeStruct((B, dil, L, C), F32)],
        compiler_params=_cparams(3), name=f"dilated_attention_d{dil}",
    )(q, k, v, tiles)


def _ffn_kernel(x_ref, g_ref, wg_ref, wu_ref, cw_ref, cb_ref, wd_ref, fg_ref, o_ref, gbuf, halo, *, tm, final):
    x = x_ref[0]
    h = _rms_bf16(x, g_ref[...])

    @pl.when(pl.program_id(1) == 0)
    def _():
        halo[...] = jnp.zeros_like(halo)

    o_ref[0] = x
    for c in range(D_FF // FF_CHUNK):
        cs = slice(FF_CHUNK * c, FF_CHUNK * (c + 1))
        gg = jnp.dot(h, wg_ref[:, cs], preferred_element_type=F32)
        gbuf[0:8] = halo[c]
        gbuf[8:8 + tm] = gg
        halo[c] = gg[tm - 8:tm]
        conv = (cw_ref[0:1, cs] * gbuf[pl.ds(6, tm), :] + cw_ref[1:2, cs] * gbuf[pl.ds(7, tm), :]
                + cw_ref[2:3, cs] * gg + cb_ref[:, cs])
        act = 0.5 * conv * (1.0 + lax.erf(conv * (2.0 ** -0.5)))
        act = act * jnp.dot(h, wu_ref[:, cs], preferred_element_type=F32)
        o_ref[0] += jnp.dot(act.astype(BF16), wd_ref[cs, :], preferred_element_type=F32)
    if final:
        y = o_ref[0]
        o_ref[0] = y * lax.rsqrt(jnp.mean(y * y, axis=-1, keepdims=True) + EPS) * fg_ref[...]


def _ffn(x, g, wg, wu, cw, cb, wd, fg, tm, final):
    B, S, D = x.shape
    row = pl.BlockSpec((1, tm, D), lambda b, i: (b, i, 0))
    const = lambda shape: _resident(shape, lambda b, i: (0,) * len(shape))
    return pl.pallas_call(
        functools.partial(_ffn_kernel, tm=tm, final=final),
        grid=(B, S // tm),
        in_specs=[row, const((1, D)), const((D, D_FF)), const((D, D_FF)), const(cw.shape), const((1, D_FF)),
                  const((D_FF, D)), const((1, D))],
        out_specs=row, out_shape=jax.ShapeDtypeStruct((B, S, D), F32),
        scratch_shapes=[pltpu.VMEM((tm + 8, FF_CHUNK), F32), pltpu.VMEM((D_FF // FF_CHUNK, 8, FF_CHUNK), F32)],
        compiler_params=_cparams(2), name="conv_glu_ffn",
    )(x, g, wg, wu, cw, cb, wd, fg)


def kernel(x, rel_bias_table, mix_norm_g, ffn_norm_g, ffn_w_gate, ffn_w_up, ffn_conv_w, ffn_conv_b, ffn_w_down,
           ev_w_in, ev_w_out, nsa_phi_pos, nsa_phi_w1, nsa_phi_w2, conv_dw_w, conv_dw_b, conv_ln_g, conv_ln_b,
           od_w_in, od_w_out, diff_lambda, diff_norm_g, final_norm_g):
    B, S, D = x.shape
    depth = mix_norm_g.shape[0]
    assert D == D_MODEL and S % 2048 == 0 and S // NSA_SEL_LEN <= HEAD_DIM
    tm = 512
    tk = tm
    n_q = S // TILE
    NC = S // NSA_CMP_STRIDE

    tsel = _causal_tiles(rel_bias_table, REL_COL_NSA, NSA_GROUPS * NSA_HPG, n_q)
    twin = _window_tiles(rel_bias_table, REL_COL_NSA, NSA_GROUPS * NSA_HPG)
    tdiff = _causal_tiles(rel_bias_table, REL_COL_DIFF, DIFF_HEADS, n_q, keys_on_rows=True)
    tdil = [_dilated_tiles(rel_bias_table, REL_COL_DIL, DIL_HEADS, dil) for _, dil in DIL_PAIRS]
    ovt = _overlap_t(S)

    for layer in range(depth):
        g_mix = mix_norm_g[layer][None, :]
        if layer % 2 == 0:
            e = layer // 2
            q, kvc, ksel, vsel, kwin, vwin, gates, glu = _even_proj(x, g_mix, _even_proj_weights(ev_w_in[e]), tm)
            r = kvc.reshape(B, NC, NSA_CMP_STRIDE, 4, HEAD_DIM).transpose(0, 3, 1, 2, 4)
            r = r.reshape(B, 4, NC, NSA_CMP_STRIDE * HEAD_DIM)
            w2 = jnp.pad(nsa_phi_w2[e], ((0, 0), (0, 0), (0, LANE - HEAD_DIM))).astype(BF16)
            cmp = _compress(r, nsa_phi_pos[e].reshape(2, 1, NSA_CMP_LEN * HEAD_DIM), nsa_phi_w1[e].astype(BF16), w2)
            front = ((0, 0), (NSA_WINDOW, 0), (0, 0))
            o_a = _nsa_attention(q, ksel, vsel, jnp.pad(kwin, front), jnp.pad(vwin, front), cmp, gates, ovt,
                                 tsel, twin, tk)
            u = _conformer(glu, conv_dw_w[e], conv_dw_b[e][None, :], conv_ln_g[e][None, :],
                           conv_ln_b[e][None, :], tm)
            x = _even_out(x, o_a, u, ev_w_out[e].astype(BF16), tm)
        else:
            o = layer // 2
            lam_init = 0.8 - 0.6 * math.exp(-0.3 * layer)
            cq, ck, cvt, *dil_qkv = _odd_proj(x, g_mix, od_w_in[o].astype(BF16), tm)
            o_c = _diff_attention(diff_lambda[o], cq, ck, cvt, tdiff, diff_norm_g[o][None, :], lam_init, tk)
            outs, lses = zip(*[_dilated_group(*dil_qkv[3 * n:3 * n + 3], tdil[n], 4) for n in range(len(DIL_PAIRS))])
            x = _odd_out(x, o_c, outs, lses, od_w_out[o].astype(BF16), tm)
        x = _ffn(x, ffn_norm_g[layer][None, :], ffn_w_gate[layer].astype(BF16), ffn_w_up[layer].astype(BF16),
                 ffn_conv_w[layer], ffn_conv_b[layer][None, :], ffn_w_down[layer].astype(BF16),
                 final_norm_g[None, :], tm, final=(layer == depth - 1))
    return x
```

```python
import functools
import math

import numpy as np
import jax
import jax.numpy as jnp
from jax import lax
from jax.experimental import pallas as pl
from jax.experimental.pallas import tpu as pltpu

F32, BF16 = jnp.float32, jnp.bfloat16

D_MODEL = 1024
HEAD_DIM = 64
HALF = 512
NSA_GROUPS = 2
NSA_HPG = 4
NSA_CMP_LEN = 32
NSA_CMP_STRIDE = 16
NSA_SEL_LEN = 64
NSA_TOP_N = 16
NSA_WINDOW = 512
NSA_FORCE = 1e6
CONV_WIDTH = 31
DIFF_HEADS = 4
DIL_HEADS = 8
DIL_PAIRS = ((128, 1), (512, 4), (2048, 16))
D_FF = 2816
REL_BUCKETS = 32
REL_MAX_DIST = 2048
REL_COL_NSA = 0
REL_COL_DIFF = 8
REL_COL_DIL = 12
EPS = 1e-6
NEG = -1e30
LOG2E = 1.4426950408889634
Q_SCALE = HEAD_DIM ** -0.5 * LOG2E

LANE = 128
TILE = 128
FF_CHUNK = 256
VMEM_LIMIT = 56 * 1024 * 1024

_Q0, _KVC0, _KSEL0, _VSEL0, _KWIN0, _VWIN0, _GATE0, _GLUA0, _GLUG0, _EV_COLS = (
    0, 1024, 1280, 1536, 1792, 2048, 2304, 2560, 3072, 3584)


def _cparams(n_axes):
    return pltpu.CompilerParams(dimension_semantics=("arbitrary",) * n_axes,
                                vmem_limit_bytes=VMEM_LIMIT)


def _resident(shape, index_map):
    return pl.BlockSpec(shape, index_map, pipeline_mode=pl.Buffered(1))


def _nt(a, b):
    return lax.dot_general(a, b, (((1,), (1,)), ((), ())), preferred_element_type=F32)


def _rms_bf16(x, g):
    ms = jnp.mean(x * x, axis=-1, keepdims=True)
    return (x * lax.rsqrt(ms + EPS) * g).astype(BF16)


def _rel_bucket(dist):
    n = jnp.maximum(dist, 0)
    max_exact = REL_BUCKETS // 2
    nf = jnp.maximum(n, 1).astype(F32)
    large = max_exact + (jnp.log(nf / max_exact) / math.log(REL_MAX_DIST / max_exact)
                         * (REL_BUCKETS - max_exact)).astype(jnp.int32)
    large = jnp.minimum(large, REL_BUCKETS - 1)
    return jnp.where(n < max_exact, n, large)


def _bias_tiles(table, col0, n_heads, dist, valid):
    bucket = _rel_bucket(jnp.asarray(dist, jnp.int32))
    onehot = (bucket[..., None] == jnp.arange(REL_BUCKETS, dtype=jnp.int32)).astype(F32)
    vals = jnp.einsum("tijb,bh->htij", onehot, table[:, col0:col0 + n_heads].astype(F32),
                      precision=lax.Precision.HIGHEST)
    return jnp.where(jnp.asarray(valid)[None], vals * LOG2E, NEG)


def _n_causal_tiles(n_q):
    half = REL_BUCKETS // 2
    n_far = math.ceil(half * (REL_MAX_DIST / half) ** ((half - 1) / half)) + 1
    d_far = (n_far + 2 * (TILE - 1)) // TILE + 1
    far = np.arange(TILE * d_far - (TILE - 1), TILE * (d_far + 2), dtype=np.float32)
    assert np.all(half + np.floor(np.log(far / half) / math.log(REL_MAX_DIST / half) * half) >= REL_BUCKETS - 1)
    return min(n_q, d_far + 1) + 1


def _causal_tiles(table, col0, n_heads, n_q):
    i = np.arange(TILE)[:, None] - np.arange(TILE)[None, :]
    dist = np.stack([np.full((TILE, TILE), -1)] + [TILE * d + i for d in range(_n_causal_tiles(n_q) - 1)])
    return _bias_tiles(table, col0, n_heads, dist, dist >= 0)


def _window_tiles(table, col0, n_heads):
    i = np.arange(TILE)[:, None] - np.arange(TILE)[None, :]
    dist = np.stack([NSA_WINDOW - TILE * b + i for b in range(NSA_WINDOW // TILE + 1)])
    return _bias_tiles(table, col0, n_heads, dist, (dist >= 0) & (dist < NSA_WINDOW))


def _dilated_tiles(table, col0, n_heads, dil):
    i = np.arange(TILE)[:, None] - np.arange(TILE)[None, :]
    sub = np.stack([TILE + i, i])
    return _bias_tiles(table, col0, n_heads, sub * dil, (sub >= 0) & (sub <= TILE))


def _even_proj_kernel(x_ref, g_ref, w_ref, q_ref, kvc_ref, ksel_ref, vsel_ref, kwin_ref, vwin_ref,
                      gate_ref, glu_ref, *, tm):
    h = _rms_bf16(x_ref[0], g_ref[...])

    def mm(a, b):
        return jnp.dot(h, w_ref[:, a:b], preferred_element_type=F32)

    q_ref[0] = (mm(_Q0, _KVC0) * Q_SCALE).astype(BF16)
    kvc_ref[0] = mm(_KVC0, _KSEL0)
    ks = mm(_KSEL0, _VSEL0)
    lane = lax.broadcasted_iota(jnp.int32, ks.shape, 1) % LANE
    pos = pl.program_id(1) * tm + lax.broadcasted_iota(jnp.int32, ks.shape, 0)
    onehot = jnp.where(lane - HEAD_DIM == pos // NSA_SEL_LEN, 1.0, 0.0)
    ksel_ref[0] = jnp.where(lane >= HEAD_DIM, onehot, ks).astype(BF16)
    vsel_ref[0] = jnp.where(lane == HEAD_DIM, 1.0, mm(_VSEL0, _KWIN0)).astype(BF16)
    kwin_ref[0] = mm(_KWIN0, _VWIN0).astype(BF16)
    vwin_ref[0] = jnp.where(lane == HEAD_DIM, 1.0, mm(_VWIN0, _GATE0)).astype(BF16)
    gate_ref[0] = jax.nn.sigmoid(mm(_GATE0, _GLUA0))
    glu_ref[0] = mm(_GLUA0, _GLUG0) * jax.nn.sigmoid(mm(_GLUG0, _EV_COLS))


def _even_proj(x, g, w, tm):
    B, S, D = x.shape
    row = lambda n: pl.BlockSpec((1, tm, n), lambda b, i: (b, i, 0))
    outs = [(1024, BF16), (256, F32), (256, BF16), (256, BF16), (256, BF16), (256, BF16), (256, F32), (512, F32)]
    return pl.pallas_call(
        functools.partial(_even_proj_kernel, tm=tm),
        grid=(B, S // tm),
        in_specs=[row(D), _resident((1, D), lambda b, i: (0, 0)), _resident((D, _EV_COLS), lambda b, i: (0, 0))],
        out_specs=[row(n) for n, _ in outs],
        out_shape=[jax.ShapeDtypeStruct((B, S, n), dt) for n, dt in outs],
        compiler_params=_cparams(2),
        name="even_proj",
    )(x, g, w)


def _even_proj_weights(w_in):
    D = w_in.shape[0]

    def slots(cols, n, width):
        p = cols.reshape(D, n, width)
        return jnp.pad(p, ((0, 0), (0, 0), (0, LANE - width))).reshape(D, n * LANE)

    kv0 = HALF
    parts = [slots(w_in[:, :HALF], 8, HEAD_DIM),
             w_in[:, kv0:kv0 + 256]]
    for br in (1, 2):
        for e in (0, 1):
            c0 = kv0 + br * 256 + e * 128
            parts.append(slots(w_in[:, c0:c0 + 128], 2, HEAD_DIM))
    g0 = kv0 + 768
    parts.append(slots(w_in[:, g0:g0 + 24], 2, 12))
    parts.append(w_in[:, g0 + 24:])
    return jnp.concatenate(parts, axis=1).astype(BF16)


def _compress_kernel(r_ref, pos_ref, w1_ref, w2_ref, o_ref):
    r = r_ref[0, 0]
    nc = r.shape[0]
    half = NSA_CMP_STRIDE * HEAD_DIM
    pos = pos_ref[0]
    lo = (r + pos[:, :half]).astype(BF16)
    hi = (pltpu.roll(r, nc - 1, 0) + pos[:, half:]).astype(BF16)
    hid = (jnp.dot(lo, w1_ref[0, :half, :], preferred_element_type=F32)
           + jnp.dot(hi, w1_ref[0, half:, :], preferred_element_type=F32))
    hid = hid * jax.nn.sigmoid(hid)
    o_ref[0, 0] = jnp.dot(hid.astype(BF16), w2_ref[0], preferred_element_type=F32).astype(BF16)


def _compress(r, pos, w1, w2):
    B, _, NC, W = r.shape
    return pl.pallas_call(
        _compress_kernel,
        grid=(B, 4),
        in_specs=[pl.BlockSpec((1, 1, NC, W), lambda b, c: (b, c, 0, 0)),
                  pl.BlockSpec((1, 1, 2 * W), lambda b, c: (c // 2, 0, 0)),
                  pl.BlockSpec((1, 2 * W, 256), lambda b, c: (c // 2, 0, 0)),
                  pl.BlockSpec((1, 256, LANE), lambda b, c: (c // 2, 0, 0))],
        out_specs=pl.BlockSpec((1, 1, NC, LANE), lambda b, c: (b, c, 0, 0)),
        out_shape=jax.ShapeDtypeStruct((B, 4, NC, LANE), BF16),
        compiler_params=_cparams(2),
        name="nsa_compress",
    )(r, pos, w1, w2)


def _stack_heads(x, n):
    return jnp.concatenate([x] * n, axis=0)


def _nsa_kernel(q_ref, ksel_ref, vsel_ref, kwin_ref, vwin_ref, cmp_ref, gate_ref, ovt_ref,
                tsel_ref, twin_ref, o_ref, acc_ref, *, tk, n_cmp, n_top, nt):
    qi = pl.program_id(1)
    s0 = qi * TILE
    H, G = NSA_HPG, NSA_GROUPS
    nc = cmp_ref.shape[2]
    wlen = NSA_WINDOW + TILE
    nb = wlen // TILE
    w0 = pl.multiple_of(s0, TILE)

    t_row = s0 + lax.broadcasted_iota(jnp.int32, (TILE, nc), 0)
    n_idx = lax.broadcasted_iota(jnp.int32, (TILE, nc), 1)
    ok = jnp.where(n_idx * NSA_CMP_STRIDE + (NSA_CMP_LEN - 1) <= t_row, 1.0, 0.0)
    ok = _stack_heads(jnp.where(n_idx < n_cmp, ok, 0.0), H)
    kpos = s0 - NSA_WINDOW + lax.broadcasted_iota(jnp.int32, (1, wlen), 1)
    wmask = jnp.where(kpos >= 0, 0.0, NEG)

    q_aug, o_fix = [], []
    for g in range(G):
        qs = jnp.concatenate([q_ref[0, :, LANE * (H * g + h):LANE * (H * g + h + 1)] for h in range(H)], axis=0)
        s_c = jnp.where(ok > 0, _nt(qs, cmp_ref[0, g]), NEG)
        p_c = jnp.exp2(s_c - jnp.max(s_c, axis=-1, keepdims=True)) * ok
        l_c = jnp.sum(p_c, axis=-1, keepdims=True)
        p_c = p_c * jnp.where(l_c > 0, 1.0 / l_c, 0.0)
        o_c = jnp.dot(p_c.astype(BF16), cmp_ref[0, G + g], preferred_element_type=F32)
        p_cat = jnp.concatenate([p_c[TILE * h:TILE * (h + 1)] for h in range(H)], axis=1).astype(BF16)
        imp = _nt(ovt_ref[...], p_cat)[HEAD_DIM:]
        nj = imp.shape[0]
        j = lax.broadcasted_iota(jnp.int32, (nj, TILE), 0)
        t = s0 + lax.broadcasted_iota(jnp.int32, (nj, TILE), 1)
        cur = t // NSA_SEL_LEN
        forced = jnp.where(j == 0, 1.0, 0.0) + jnp.where(j == cur, 1.0, 0.0) + jnp.where(j == cur - 1, 1.0, 0.0)
        imp = jnp.where(forced > 0, NSA_FORCE, imp)
        imp = jnp.where(j * NSA_SEL_LEN <= t, imp, NEG)
        rank = jnp.zeros((nj, TILE), F32)
        for k in range(nj):
            vk = imp[k:k + 1, :]
            rank = rank + jnp.where(j > k, jnp.where(vk >= imp, 1.0, 0.0), jnp.where(vk > imp, 1.0, 0.0))
        mneg = jnp.where(rank < n_top, 0.0, NEG)
        m_t = jnp.concatenate([jnp.zeros((LANE - nj, TILE), F32), mneg], axis=0)
        q_aug.append(qs + _stack_heads(m_t.T.astype(BF16), H))
        s_w = _nt(qs, kwin_ref[0, pl.ds(w0, wlen), LANE * g:LANE * (g + 1)])
        s_w = s_w + jnp.concatenate(
            [jnp.concatenate([twin_ref[H * g + h, b] for b in range(nb)], axis=1) for h in range(H)], axis=0)
        s_w = s_w + wmask
        p_w = jnp.exp2(s_w - jnp.max(s_w, axis=-1, keepdims=True))
        o_w = jnp.dot(p_w.astype(BF16), vwin_ref[0, pl.ds(w0, wlen), LANE * g:LANE * (g + 1)],
                      preferred_element_type=F32)
        o_w = o_w / o_w[:, HEAD_DIM:HEAD_DIM + 1]
        gt = gate_ref[0, :, LANE * g:LANE * (g + 1)]
        gate = lambda br: jnp.concatenate([gt[:, 3 * h + br:3 * h + br + 1] for h in range(H)], axis=0)
        o_fix.append((gate(0) * o_c + gate(2) * o_w, gate(1)))

    sub = tk // TILE

    def body(kc, carry):
        k0 = pl.multiple_of(kc * tk, tk)
        idx = [jnp.minimum(jnp.maximum(qi - (kc * sub + c) + 1, 0), nt - 1) for c in range(sub)]
        out = []
        for g in range(G):
            m = carry[g]
            s = _nt(q_aug[g], ksel_ref[0, pl.ds(k0, tk), LANE * g:LANE * (g + 1)])
            s = s + jnp.concatenate(
                [jnp.concatenate([tsel_ref[H * g + h, idx[c]] for c in range(sub)], axis=1) for h in range(H)], axis=0)
            m_new = jnp.maximum(m, jnp.max(s, axis=-1, keepdims=True))
            alpha = jnp.exp2(m - m_new)
            p = jnp.exp2(s - m_new)
            acc_ref[g] = acc_ref[g] * alpha + jnp.dot(
                p.astype(BF16), vsel_ref[0, pl.ds(k0, tk), LANE * g:LANE * (g + 1)], preferred_element_type=F32)
            out.append(m_new)
        return tuple(out)

    acc_ref[...] = jnp.zeros_like(acc_ref)
    lax.fori_loop(0, qi // sub + 1, body, (jnp.full((H * TILE, 1), NEG, F32),) * G)
    outs = []
    for g in range(G):
        acc = acc_ref[g]
        o = o_fix[g][0] + o_fix[g][1] * (acc / acc[:, HEAD_DIM:HEAD_DIM + 1])
        outs += [o[TILE * h:TILE * (h + 1), :HEAD_DIM] for h in range(H)]
    o_ref[0] = jnp.concatenate(outs, axis=1).astype(BF16)


def _nsa_attention(q, ksel, vsel, kwin, vwin, cmp, gates, ovt, tsel, twin, tk):
    B, S, _ = q.shape
    n_q = S // TILE
    full = lambda a: pl.BlockSpec((1,) + a.shape[1:], lambda b, i: (b,) + (0,) * (a.ndim - 1))
    return pl.pallas_call(
        functools.partial(_nsa_kernel, tk=tk, n_cmp=S // NSA_CMP_STRIDE - 1,
                          n_top=min(NSA_TOP_N, S // NSA_SEL_LEN), nt=tsel.shape[1]),
        grid=(B, n_q),
        in_specs=[pl.BlockSpec((1, TILE, q.shape[2]), lambda b, i: (b, i, 0)),
                  full(ksel), full(vsel), full(kwin), full(vwin), full(cmp),
                  pl.BlockSpec((1, TILE, gates.shape[2]), lambda b, i: (b, i, 0)),
                  _resident(ovt.shape, lambda b, i: (0, 0)),
                  _resident(tsel.shape, lambda b, i: (0, 0, 0, 0)),
                  _resident(twin.shape, lambda b, i: (0, 0, 0, 0))],
        out_specs=pl.BlockSpec((1, TILE, HALF), lambda b, i: (b, i, 0)),
        out_shape=jax.ShapeDtypeStruct((B, S, HALF), BF16),
        scratch_shapes=[pltpu.VMEM((NSA_GROUPS, NSA_HPG * TILE, LANE), F32)],
        compiler_params=_cparams(2),
        name="nsa_attention",
    )(q, ksel, vsel, kwin, vwin, cmp, gates, ovt, tsel, twin)


def _overlap_t(S):
    n_cmp = (S - NSA_CMP_LEN) // NSA_CMP_STRIDE + 1
    n_sel = S // NSA_SEL_LEN
    cmp_idx = np.arange(n_cmp)[:, None] * NSA_CMP_STRIDE + np.arange(NSA_CMP_LEN)[None, :]
    ov = (cmp_idx[:, :, None] // NSA_SEL_LEN == np.arange(n_sel)[None, None, :]).mean(axis=1)
    out = np.zeros((LANE, S // NSA_CMP_STRIDE), np.float32)
    out[HEAD_DIM:HEAD_DIM + n_sel, :n_cmp] = ov.T
    return jnp.asarray(np.tile(out, (1, NSA_HPG)), BF16)


def _conformer_kernel(x_ref, w_ref, b_ref, g_ref, beta_ref, o_ref, buf, xs, *, tm):
    halo = 32

    @pl.when(pl.program_id(1) == 0)
    def _():
        buf[0:halo] = jnp.zeros((halo, buf.shape[1]), F32)

    buf[halo:halo + tm] = x_ref[0]
    n = tm + halo - 8
    for ph in range(1, 8):
        xs[ph, 0:n] = buf[pl.ds(ph, n), :]
    acc = jnp.zeros((tm, buf.shape[1]), F32) + b_ref[...]
    for j in range(CONV_WIDTH):
        off = halo - (CONV_WIDTH - 1) + j
        a, ph = off // 8, off % 8
        tap = buf[pl.ds(8 * a, tm), :] if ph == 0 else xs[ph, pl.ds(8 * a, tm), :]
        acc = acc + w_ref[j:j + 1, :] * tap
    mu = jnp.mean(acc, axis=-1, keepdims=True)
    var = jnp.mean(jnp.square(acc - mu), axis=-1, keepdims=True)
    y = (acc - mu) * lax.rsqrt(var + EPS) * g_ref[...] + beta_ref[...]
    o_ref[0] = (y * jax.nn.sigmoid(y)).astype(BF16)
    buf[0:halo] = buf[tm:tm + halo]


def _conformer(glu, w, b, g, beta, tm):
    B, S, C = glu.shape
    vec = lambda r: pl.BlockSpec((r, C), lambda bb, i: (0, 0))
    return pl.pallas_call(
        functools.partial(_conformer_kernel, tm=tm),
        grid=(B, S // tm),
        in_specs=[pl.BlockSpec((1, tm, C), lambda bb, i: (bb, i, 0)), vec(CONV_WIDTH), vec(1), vec(1), vec(1)],
        out_specs=pl.BlockSpec((1, tm, C), lambda bb, i: (bb, i, 0)),
        out_shape=jax.ShapeDtypeStruct((B, S, C), BF16),
        scratch_shapes=[pltpu.VMEM((tm + 32, C), F32), pltpu.VMEM((8, tm + 32, C), F32)],
        compiler_params=_cparams(2),
        name="conformer_conv",
    )(glu, w, b, g, beta)


def _even_out_kernel(x_ref, a_ref, u_ref, w_ref, o_ref):
    o_ref[0] = (x_ref[0] + jnp.dot(a_ref[0], w_ref[:HALF, :], preferred_element_type=F32)
                + jnp.dot(u_ref[0], w_ref[HALF:, :], preferred_element_type=F32))


def _even_out(x, a, u, w, tm):
    B, S, D = x.shape
    row = lambda n: pl.BlockSpec((1, tm, n), lambda b, i: (b, i, 0))
    return pl.pallas_call(
        _even_out_kernel, grid=(B, S // tm),
        in_specs=[row(D), row(HALF), row(HALF), _resident((D, D), lambda b, i: (0, 0))],
        out_specs=row(D), out_shape=jax.ShapeDtypeStruct((B, S, D), F32),
        compiler_params=_cparams(2), name="even_out",
    )(x, a, u, w)


def _odd_out_kernel(x_ref, c_ref, o1_ref, o4_ref, o16_ref, l1_ref, l4_ref, l16_ref, w_ref, o_ref,
                    b_o4, b_o16, b_l4, b_l16, *, tm):
    for src, dst, d in ((o4_ref, b_o4, 4), (l4_ref, b_l4, 4), (o16_ref, b_o16, 16), (l16_ref, b_l16, 16)):
        for r in range(d):
            blk = src[0, r].astype(F32)
            for c in range(HALF // LANE):
                dst[c, pl.ds(r, tm // d, stride=d), :] = blk[:, LANE * c:LANE * (c + 1)]
    full = lambda buf: jnp.concatenate([buf[c] for c in range(HALF // LANE)], axis=1)
    l1, l2, l3 = l1_ref[0, 0], full(b_l4), full(b_l16)
    m = jnp.maximum(jnp.maximum(l1, l2), l3)
    e1, e2, e3 = jnp.exp2(l1 - m), jnp.exp2(l2 - m), jnp.exp2(l3 - m)
    od = (e1 * o1_ref[0, 0] + e2 * full(b_o4) + e3 * full(b_o16)) / (e1 + e2 + e3)
    o_ref[0] = (x_ref[0] + jnp.dot(c_ref[0], w_ref[:HALF, :], preferred_element_type=F32)
                + jnp.dot(od.astype(BF16), w_ref[HALF:, :], preferred_element_type=F32))


def _odd_out(x, oc, os_, ls, w, tm):
    B, S, D = x.shape
    row = lambda n: pl.BlockSpec((1, tm, n), lambda b, i: (b, i, 0))
    dil = lambda d: pl.BlockSpec((1, d, tm // d, HALF), lambda b, i: (b, 0, i, 0))
    return pl.pallas_call(
        functools.partial(_odd_out_kernel, tm=tm), grid=(B, S // tm),
        in_specs=[row(D), row(HALF), dil(1), dil(4), dil(16), dil(1), dil(4), dil(16),
                  _resident((D, D), lambda b, i: (0, 0))],
        out_specs=row(D), out_shape=jax.ShapeDtypeStruct((B, S, D), F32),
        scratch_shapes=[pltpu.VMEM((HALF // LANE, tm, LANE), F32)] * 4,
        compiler_params=_cparams(2), name="odd_out",
    )(x, oc, *os_, *ls, w)


def _odd_proj_kernel(x_ref, g_ref, w_ref, cq_ref, ck_ref, cv_ref, q1_ref, k1_ref, v1_ref,
                     q4_ref, k4_ref, v4_ref, q16_ref, k16_ref, v16_ref, ybuf, *, tm):
    h = _rms_bf16(x_ref[0], g_ref[...])
    scale = Q_SCALE
    for n, (ref, s) in enumerate(((cq_ref, scale), (ck_ref, 1.0), (cv_ref, 1.0))):
        y = jnp.dot(h, w_ref[:, HALF * n:HALF * (n + 1)], preferred_element_type=F32)
        ref[0] = (y * s).astype(BF16)
    for n, (r1, r4, r16, s) in enumerate(((q1_ref, q4_ref, q16_ref, scale), (k1_ref, k4_ref, k16_ref, 1.0),
                                          (v1_ref, v4_ref, v16_ref, 1.0))):
        y = jnp.dot(h, w_ref[:, HALF * (n + 3):HALF * (n + 4)], preferred_element_type=F32) * s
        r1[0, 0] = y.astype(BF16)
        for c in range(HALF // LANE):
            ybuf[c] = y[:, LANE * c:LANE * (c + 1)]
        for d, ref in ((4, r4), (16, r16)):
            for r in range(d):
                ref[0, r] = jnp.concatenate([ybuf[c, pl.ds(r, tm // d, stride=d), :] for c in range(HALF // LANE)],
                                            axis=1).astype(BF16)


def _odd_proj(x, g, w, tm):
    B, S, D = x.shape
    row = lambda n: pl.BlockSpec((1, tm, n), lambda b, i: (b, i, 0))
    dil = lambda d: pl.BlockSpec((1, d, tm // d, HALF), lambda b, i: (b, 0, i, 0))
    dshape = lambda d: jax.ShapeDtypeStruct((B, d, S // d, HALF), BF16)
    return pl.pallas_call(
        functools.partial(_odd_proj_kernel, tm=tm), grid=(B, S // tm),
        in_specs=[row(D), _resident((1, D), lambda b, i: (0, 0)), _resident((D, 6 * HALF), lambda b, i: (0, 0))],
        out_specs=[row(HALF)] * 3 + [dil(1)] * 3 + [dil(4)] * 3 + [dil(16)] * 3,
        out_shape=[jax.ShapeDtypeStruct((B, S, HALF), BF16)] * 3 + [dshape(1)] * 3 + [dshape(4)] * 3 + [dshape(16)] * 3,
        scratch_shapes=[pltpu.VMEM((HALF // LANE, tm, LANE), F32)],
        compiler_params=_cparams(2), name="odd_proj",
    )(x, g, w)


def _diff_kernel(lp_ref, q_ref, k_ref, v_ref, t_ref, g_ref, o_ref, acc_ref, *, tk, lam_init, nt):
    qi = pl.program_id(1)
    H = DIFF_HEADS
    lp = lp_ref[...]
    lam = (jnp.exp(jnp.sum(lp[0:1] * lp[1:2], axis=-1, keepdims=True))
           - jnp.exp(jnp.sum(lp[2:3] * lp[3:4], axis=-1, keepdims=True)) + lam_init)
    lane = lax.broadcasted_iota(jnp.int32, (TILE, LANE), 1)
    qs = []
    for h in range(H):
        q = q_ref[0, :, LANE * h:LANE * (h + 1)]
        zero = jnp.zeros_like(q)
        qs.append(jnp.concatenate([jnp.where(lane < HEAD_DIM, q, zero), jnp.where(lane >= HEAD_DIM, q, zero)], axis=0))
    sub = tk // TILE
    ones_col = jnp.where(lax.broadcasted_iota(jnp.int32, (tk, LANE), 1) == 0, 1.0, 0.0).astype(BF16)

    def body(kc, carry):
        k0 = pl.multiple_of(kc * tk, tk)
        idx = [jnp.minimum(jnp.maximum(qi - (kc * sub + c) + 1, 0), nt - 1) for c in range(sub)]
        out = []
        for h in range(H):
            m = carry[h]
            s = _nt(qs[h], k_ref[0, pl.ds(k0, tk), LANE * h:LANE * (h + 1)])
            bt = jnp.concatenate([t_ref[h, idx[c]] for c in range(sub)], axis=1)
            s = s + _stack_heads(bt, 2)
            m_new = jnp.maximum(m, jnp.max(s, axis=-1, keepdims=True))
            alpha = jnp.exp2(m - m_new)
            p = jnp.exp2(s - m_new)
            v_aug = jnp.concatenate([v_ref[0, pl.ds(k0, tk), LANE * h:LANE * (h + 1)], ones_col], axis=1)
            acc_ref[h] = acc_ref[h] * alpha + jnp.dot(p.astype(BF16), v_aug, preferred_element_type=F32)
            out.append(m_new)
        return tuple(out)

    acc_ref[...] = jnp.zeros_like(acc_ref)
    lax.fori_loop(0, qi // sub + 1, body, (jnp.full((2 * TILE, 1), NEG, F32),) * H)
    outs = []
    for h in range(H):
        acc = acc_ref[h]
        on = acc[:, :LANE] / acc[:, LANE:LANE + 1]
        o = on[:TILE] - lam * on[TILE:]
        o = o * lax.rsqrt(jnp.mean(o * o, axis=-1, keepdims=True) + EPS) * g_ref[...] * (1.0 - lam_init)
        outs.append(o.astype(BF16))
    o_ref[0] = jnp.concatenate(outs, axis=1)


def _diff_attention(lp, q, k, v, tiles, g, lam_init, tk):
    B, S, _ = q.shape
    n_q = S // TILE
    return pl.pallas_call(
        functools.partial(_diff_kernel, tk=tk, lam_init=lam_init, nt=tiles.shape[1]),
        grid=(B, n_q),
        in_specs=[pl.BlockSpec(lp.shape, lambda b, i: (0, 0)),
                  pl.BlockSpec((1, TILE, HALF), lambda b, i: (b, i, 0)),
                  pl.BlockSpec((1, S, HALF), lambda b, i: (b, 0, 0)),
                  pl.BlockSpec((1, S, HALF), lambda b, i: (b, 0, 0)),
                  _resident(tiles.shape, lambda b, i: (0, 0, 0, 0)),
                  pl.BlockSpec(g.shape, lambda b, i: (0, 0))],
        out_specs=pl.BlockSpec((1, TILE, HALF), lambda b, i: (b, i, 0)),
        out_shape=jax.ShapeDtypeStruct((B, S, HALF), BF16),
        scratch_shapes=[pltpu.VMEM((DIFF_HEADS, 2 * TILE, 2 * LANE), F32)],
        compiler_params=_cparams(2), name="diff_attention",
    )(lp, q, k, v, tiles, g)


def _dilated_kernel(q_ref, k_ref, v_ref, t_ref, o_ref, lse_ref, *, nqb):
    n0 = pl.program_id(2) * nqb
    H = 4
    W = H * HEAD_DIM
    lane = lax.broadcasted_iota(jnp.int32, (TILE, W), 1) // HEAD_DIM
    col = lax.broadcasted_iota(jnp.int32, (1, 2 * TILE), 1)
    for qb in range(nqb):
        n = n0 + qb
        cur = pl.multiple_of(n * TILE, TILE)
        prev = pl.multiple_of(jnp.maximum(n - 1, 0) * TILE, TILE)
        pmask = jnp.where((col < TILE) & (n == 0), NEG, 0.0)
        outs, lses = [], []
        for c in range(2):
            cs = slice(W * c, W * (c + 1))
            q = q_ref[0, 0, pl.ds(qb * TILE, TILE), cs]
            zero = jnp.zeros_like(q)
            qs = jnp.concatenate([jnp.where(lane == h, q, zero) for h in range(H)], axis=0)
            kcat = jnp.concatenate([k_ref[0, 0, pl.ds(prev, TILE), cs], k_ref[0, 0, pl.ds(cur, TILE), cs]], axis=0)
            vcat = jnp.concatenate([v_ref[0, 0, pl.ds(prev, TILE), cs], v_ref[0, 0, pl.ds(cur, TILE), cs]], axis=0)
            s = _nt(qs, kcat)
            s = s + jnp.concatenate(
                [jnp.concatenate([t_ref[H * c + h, 0], t_ref[H * c + h, 1]], axis=1) for h in range(H)], axis=0)
            s = s + pmask
            m = jnp.max(s, axis=-1, keepdims=True)
            p = jnp.exp2(s - m)
            l = jnp.sum(p, axis=-1, keepdims=True)
            r = jnp.dot(p.astype(BF16), vcat, preferred_element_type=F32) / l
            lse = m + jnp.log2(l)
            out = jnp.zeros((TILE, W), F32)
            lse_out = jnp.zeros((TILE, W), F32)
            for h in range(H):
                out = jnp.where(lane == h, r[TILE * h:TILE * (h + 1)], out)
                lse_out = jnp.where(lane == h, lse[TILE * h:TILE * (h + 1)], lse_out)
            outs.append(out.astype(BF16))
            lses.append(lse_out)
        o_ref[0, 0, pl.ds(qb * TILE, TILE), :] = jnp.concatenate(outs, axis=1)
        lse_ref[0, 0, pl.ds(qb * TILE, TILE), :] = jnp.concatenate(lses, axis=1)


def _dilated_group(q, k, v, tiles, nqb):
    B, dil, L, C = q.shape
    nqb = min(nqb, L // TILE)
    qb = pl.BlockSpec((1, 1, nqb * TILE, C), lambda b, r, n: (b, r, n, 0))
    kvb = pl.BlockSpec((1, 1, L, C), lambda b, r, n: (b, r, 0, 0))
    return pl.pallas_call(
        functools.partial(_dilated_kernel, nqb=nqb),
        grid=(B, dil, L // (nqb * TILE)),
        in_specs=[qb, kvb, kvb, _resident(tiles.shape, lambda b, r, n: (0, 0, 0, 0))],
        out_specs=[qb, qb],
        out_shape=[jax.ShapeDtypeStruct((B, dil, L, C), BF16), jax.ShapeDtypeStruct((B, dil, L, C), F32)],
        compiler_params=_cparams(3), name=f"dilated_attention_d{dil}",
    )(q, k, v, tiles)


def _ffn_kernel(x_ref, g_ref, wg_ref, wu_ref, cw_ref, cb_ref, wd_ref, fg_ref, o_ref, gbuf, halo, *, tm, final):
    x = x_ref[0]
    h = _rms_bf16(x, g_ref[...])

    @pl.when(pl.program_id(1) == 0)
    def _():
        halo[...] = jnp.zeros_like(halo)

    n = D_FF // FF_CHUNK
    cs_of = lambda c: slice(FF_CHUNK * c, FF_CHUNK * (c + 1))
    gate_up = lambda c: (jnp.dot(h, wg_ref[:, cs_of(c)], preferred_element_type=F32),
                         jnp.dot(h, wu_ref[:, cs_of(c)], preferred_element_type=F32))
    o_ref[0] = x
    nxt = gate_up(0)
    for c in range(n):
        cs = cs_of(c)
        gg, uu = nxt
        if c + 1 < n:
            nxt = gate_up(c + 1)
        gbuf[c % 2, 0:8] = halo[c]
        gbuf[c % 2, 8:8 + tm] = gg
        halo[c] = gg[tm - 8:tm]
        conv = (cw_ref[0:1, cs] * gbuf[c % 2, pl.ds(6, tm), :] + cw_ref[1:2, cs] * gbuf[c % 2, pl.ds(7, tm), :]
                + cw_ref[2:3, cs] * gg + cb_ref[:, cs])
        act = 0.5 * conv * (1.0 + lax.erf(conv * (2.0 ** -0.5))) * uu
        o_ref[0] += jnp.dot(act.astype(BF16), wd_ref[cs, :], preferred_element_type=F32)
    if final:
        y = o_ref[0]
        o_ref[0] = y * lax.rsqrt(jnp.mean(y * y, axis=-1, keepdims=True) + EPS) * fg_ref[...]


def _ffn(x, g, wg, wu, cw, cb, wd, fg, tm, final):
    B, S, D = x.shape
    row = pl.BlockSpec((1, tm, D), lambda b, i: (b, i, 0))
    const = lambda shape: _resident(shape, lambda b, i: (0,) * len(shape))
    return pl.pallas_call(
        functools.partial(_ffn_kernel, tm=tm, final=final),
        grid=(B, S // tm),
        in_specs=[row, const((1, D)), const((D, D_FF)), const((D, D_FF)), const(cw.shape), const((1, D_FF)),
                  const((D_FF, D)), const((1, D))],
        out_specs=row, out_shape=jax.ShapeDtypeStruct((B, S, D), F32),
        scratch_shapes=[pltpu.VMEM((2, tm + 8, FF_CHUNK), F32), pltpu.VMEM((D_FF // FF_CHUNK, 8, FF_CHUNK), F32)],
        compiler_params=_cparams(2), name="conv_glu_ffn",
    )(x, g, wg, wu, cw, cb, wd, fg)


def kernel(x, rel_bias_table, mix_norm_g, ffn_norm_g, ffn_w_gate, ffn_w_up, ffn_conv_w, ffn_conv_b, ffn_w_down,
           ev_w_in, ev_w_out, nsa_phi_pos, nsa_phi_w1, nsa_phi_w2, conv_dw_w, conv_dw_b, conv_ln_g, conv_ln_b,
           od_w_in, od_w_out, diff_lambda, diff_norm_g, final_norm_g):
    B, S, D = x.shape
    depth = mix_norm_g.shape[0]
    assert D == D_MODEL and S % 2048 == 0 and S // NSA_SEL_LEN <= HEAD_DIM
    tm = 512
    tk = 512
    n_q = S // TILE
    NC = S // NSA_CMP_STRIDE

    tsel = _causal_tiles(rel_bias_table, REL_COL_NSA, NSA_GROUPS * NSA_HPG, n_q)
    twin = _window_tiles(rel_bias_table, REL_COL_NSA, NSA_GROUPS * NSA_HPG)
    tdiff = _causal_tiles(rel_bias_table, REL_COL_DIFF, DIFF_HEADS, n_q)
    tdil = [_dilated_tiles(rel_bias_table, REL_COL_DIL, DIL_HEADS, dil) for _, dil in DIL_PAIRS]
    ovt = _overlap_t(S)

    for layer in range(depth):
        g_mix = mix_norm_g[layer][None, :]
        if layer % 2 == 0:
            e = layer // 2
            q, kvc, ksel, vsel, kwin, vwin, gates, glu = _even_proj(x, g_mix, _even_proj_weights(ev_w_in[e]), tm)
            r = kvc.reshape(B, NC, NSA_CMP_STRIDE, 4, HEAD_DIM).transpose(0, 3, 1, 2, 4)
            r = r.reshape(B, 4, NC, NSA_CMP_STRIDE * HEAD_DIM)
            w2 = jnp.pad(nsa_phi_w2[e], ((0, 0), (0, 0), (0, LANE - HEAD_DIM))).astype(BF16)
            cmp = _compress(r, nsa_phi_pos[e].reshape(2, 1, NSA_CMP_LEN * HEAD_DIM), nsa_phi_w1[e].astype(BF16), w2)
            front = ((0, 0), (NSA_WINDOW, 0), (0, 0))
            o_a = _nsa_attention(q, ksel, vsel, jnp.pad(kwin, front), jnp.pad(vwin, front), cmp, gates, ovt,
                                 tsel, twin, tk)
            u = _conformer(glu, conv_dw_w[e], conv_dw_b[e][None, :], conv_ln_g[e][None, :],
                           conv_ln_b[e][None, :], tm)
            x = _even_out(x, o_a, u, ev_w_out[e].astype(BF16), tm)
        else:
            o = layer // 2
            lam_init = 0.8 - 0.6 * math.exp(-0.3 * layer)
            cq, ck, cv, *dil_qkv = _odd_proj(x, g_mix, od_w_in[o].astype(BF16), tm)
            o_c = _diff_attention(diff_lambda[o], cq, ck, cv, tdiff, diff_norm_g[o][None, :], lam_init, tk)
            outs, lses = zip(*[_dilated_group(*dil_qkv[3 * n:3 * n + 3], tdil[n], 4) for n in range(len(DIL_PAIRS))])
            x = _odd_out(x, o_c, outs, lses, od_w_out[o].astype(BF16), tm)
        x = _ffn(x, ffn_norm_g[layer][None, :], ffn_w_gate[layer].astype(BF16), ffn_w_up[layer].astype(BF16),
                 ffn_conv_w[layer], ffn_conv_b[layer][None, :], ffn_w_down[layer].astype(BF16),
                 final_norm_g[None, :], tm, final=(layer == depth - 1))
    return x
```

```python
import functools
import math

import numpy as np
import jax
import jax.numpy as jnp
from jax import lax
from jax.experimental import pallas as pl
from jax.experimental.pallas import tpu as pltpu

F32, BF16 = jnp.float32, jnp.bfloat16

D_MODEL = 1024
HEAD_DIM = 64
HALF = 512
NSA_GROUPS = 2
NSA_HPG = 4
NSA_CMP_LEN = 32
NSA_CMP_STRIDE = 16
NSA_SEL_LEN = 64
NSA_TOP_N = 16
NSA_WINDOW = 512
NSA_FORCE = 1e6
CONV_WIDTH = 31
DIFF_HEADS = 4
DIL_HEADS = 8
DIL_PAIRS = ((128, 1), (512, 4), (2048, 16))
D_FF = 2816
REL_BUCKETS = 32
REL_MAX_DIST = 2048
REL_COL_NSA = 0
REL_COL_DIFF = 8
REL_COL_DIL = 12
EPS = 1e-6
NEG = -1e30
LOG2E = 1.4426950408889634
Q_SCALE = HEAD_DIM ** -0.5 * LOG2E

LANE = 128
TILE = 128
FF_CHUNK = 1024
VMEM_LIMIT = 56 * 1024 * 1024

_Q0, _KVC0, _KSEL0, _VSEL0, _KWIN0, _VWIN0, _GATE0, _GLUA0, _GLUG0, _EV_COLS = (
    0, 1024, 1280, 1536, 1792, 2048, 2304, 2560, 3072, 3584)


def _cparams(n_axes):
    return pltpu.CompilerParams(dimension_semantics=("arbitrary",) * n_axes,
                                vmem_limit_bytes=VMEM_LIMIT)


def _resident(shape, index_map):
    return pl.BlockSpec(shape, index_map, pipeline_mode=pl.Buffered(1))


def _nt(a, b):
    return lax.dot_general(a, b, (((1,), (1,)), ((), ())), preferred_element_type=F32)


def _rms_bf16(x, g):
    ms = jnp.mean(x * x, axis=-1, keepdims=True)
    return (x * lax.rsqrt(ms + EPS) * g).astype(BF16)


def _rel_bucket(dist):
    n = jnp.maximum(dist, 0)
    max_exact = REL_BUCKETS // 2
    nf = jnp.maximum(n, 1).astype(F32)
    large = max_exact + (jnp.log(nf / max_exact) / math.log(REL_MAX_DIST / max_exact)
                         * (REL_BUCKETS - max_exact)).astype(jnp.int32)
    large = jnp.minimum(large, REL_BUCKETS - 1)
    return jnp.where(n < max_exact, n, large)


def _bias_tiles(table, col0, n_heads, dist, valid):
    bucket = _rel_bucket(jnp.asarray(dist, jnp.int32))
    onehot = (bucket[..., None] == jnp.arange(REL_BUCKETS, dtype=jnp.int32)).astype(F32)
    vals = jnp.einsum("tijb,bh->htij", onehot, table[:, col0:col0 + n_heads].astype(F32),
                      precision=lax.Precision.HIGHEST)
    return jnp.where(jnp.asarray(valid)[None], vals * LOG2E, NEG)


def _n_causal_tiles(n_q):
    half = REL_BUCKETS // 2
    n_far = math.ceil(half * (REL_MAX_DIST / half) ** ((half - 1) / half)) + 1
    d_far = (n_far + 2 * (TILE - 1)) // TILE + 1
    far = np.arange(TILE * d_far - (TILE - 1), TILE * (d_far + 2), dtype=np.float32)
    assert np.all(half + np.floor(np.log(far / half) / math.log(REL_MAX_DIST / half) * half) >= REL_BUCKETS - 1)
    return min(n_q, d_far + 1) + 1


def _causal_tiles(table, col0, n_heads, n_q):
    i = np.arange(TILE)[:, None] - np.arange(TILE)[None, :]
    dist = np.stack([np.full((TILE, TILE), -1)] + [TILE * d + i for d in range(_n_causal_tiles(n_q) - 1)])
    return _bias_tiles(table, col0, n_heads, dist, dist >= 0)


def _window_tiles(table, col0, n_heads):
    i = np.arange(TILE)[:, None] - np.arange(TILE)[None, :]
    dist = np.stack([NSA_WINDOW - TILE * b + i for b in range(NSA_WINDOW // TILE + 1)] + [np.full((TILE, TILE), -1)])
    return _bias_tiles(table, col0, n_heads, dist, (dist >= 0) & (dist < NSA_WINDOW))


def _dilated_tiles(table, col0, n_heads, dil):
    i = np.arange(TILE)[:, None] - np.arange(TILE)[None, :]
    sub = np.stack([TILE + i, i])
    return _bias_tiles(table, col0, n_heads, sub * dil, (sub >= 0) & (sub <= TILE))


def _even_proj_kernel(x_ref, g_ref, w_ref, q_ref, kvc_ref, ksel_ref, vsel_ref, kwin_ref, vwin_ref,
                      gate_ref, glu_ref, *, tm):
    h = _rms_bf16(x_ref[0], g_ref[...])

    def mm(a, b):
        return jnp.dot(h, w_ref[:, a:b], preferred_element_type=F32)

    q_ref[0] = (mm(_Q0, _KVC0) * Q_SCALE).astype(BF16)
    kvc_ref[0] = mm(_KVC0, _KSEL0)
    ks = mm(_KSEL0, _VSEL0)
    lane = lax.broadcasted_iota(jnp.int32, ks.shape, 1) % LANE
    pos = pl.program_id(1) * tm + lax.broadcasted_iota(jnp.int32, ks.shape, 0)
    onehot = jnp.where(lane - HEAD_DIM == pos // NSA_SEL_LEN, 1.0, 0.0)
    ksel_ref[0] = jnp.where(lane >= HEAD_DIM, onehot, ks).astype(BF16)
    vsel_ref[0] = jnp.where(lane == HEAD_DIM, 1.0, mm(_VSEL0, _KWIN0)).astype(BF16)
    kwin_ref[0] = mm(_KWIN0, _VWIN0).astype(BF16)
    vwin_ref[0] = jnp.where(lane == HEAD_DIM, 1.0, mm(_VWIN0, _GATE0)).astype(BF16)
    gate_ref[0] = jax.nn.sigmoid(mm(_GATE0, _GLUA0))
    glu_ref[0] = mm(_GLUA0, _GLUG0) * jax.nn.sigmoid(mm(_GLUG0, _EV_COLS))


def _even_proj(x, g, w, tm):
    B, S, D = x.shape
    row = lambda n: pl.BlockSpec((1, tm, n), lambda b, i: (b, i, 0))
    outs = [(1024, BF16), (256, F32), (256, BF16), (256, BF16), (256, BF16), (256, BF16), (256, F32), (512, F32)]
    return pl.pallas_call(
        functools.partial(_even_proj_kernel, tm=tm),
        grid=(B, S // tm),
        in_specs=[row(D), _resident((1, D), lambda b, i: (0, 0)), _resident((D, _EV_COLS), lambda b, i: (0, 0))],
        out_specs=[row(n) for n, _ in outs],
        out_shape=[jax.ShapeDtypeStruct((B, S, n), dt) for n, dt in outs],
        compiler_params=_cparams(2),
        name="even_proj",
    )(x, g, w)


def _even_proj_weights(w_in):
    D = w_in.shape[0]

    def slots(cols, n, width):
        p = cols.reshape(D, n, width)
        return jnp.pad(p, ((0, 0), (0, 0), (0, LANE - width))).reshape(D, n * LANE)

    kv0 = HALF
    parts = [slots(w_in[:, :HALF], 8, HEAD_DIM),
             w_in[:, kv0:kv0 + 256]]
    for br in (1, 2):
        for e in (0, 1):
            c0 = kv0 + br * 256 + e * 128
            parts.append(slots(w_in[:, c0:c0 + 128], 2, HEAD_DIM))
    g0 = kv0 + 768
    parts.append(slots(w_in[:, g0:g0 + 24], 2, 12))
    parts.append(w_in[:, g0 + 24:])
    return jnp.concatenate(parts, axis=1).astype(BF16)


def _compress_kernel(r_ref, pos_ref, w1_ref, w2_ref, o_ref):
    r = r_ref[0, 0]
    nc = r.shape[0]
    half = NSA_CMP_STRIDE * HEAD_DIM
    pos = pos_ref[0]
    lo = (r + pos[:, :half]).astype(BF16)
    hi = (pltpu.roll(r, nc - 1, 0) + pos[:, half:]).astype(BF16)
    hid = (jnp.dot(lo, w1_ref[0, :half, :], preferred_element_type=F32)
           + jnp.dot(hi, w1_ref[0, half:, :], preferred_element_type=F32))
    hid = hid * jax.nn.sigmoid(hid)
    o_ref[0, 0] = jnp.dot(hid.astype(BF16), w2_ref[0], preferred_element_type=F32).astype(BF16)


def _compress(r, pos, w1, w2):
    B, _, NC, W = r.shape
    return pl.pallas_call(
        _compress_kernel,
        grid=(B, 4),
        in_specs=[pl.BlockSpec((1, 1, NC, W), lambda b, c: (b, c, 0, 0)),
                  pl.BlockSpec((1, 1, 2 * W), lambda b, c: (c // 2, 0, 0)),
                  pl.BlockSpec((1, 2 * W, 256), lambda b, c: (c // 2, 0, 0)),
                  pl.BlockSpec((1, 256, LANE), lambda b, c: (c // 2, 0, 0))],
        out_specs=pl.BlockSpec((1, 1, NC, LANE), lambda b, c: (b, c, 0, 0)),
        out_shape=jax.ShapeDtypeStruct((B, 4, NC, LANE), BF16),
        compiler_params=_cparams(2),
        name="nsa_compress",
    )(r, pos, w1, w2)


def _stack_heads(x, n):
    return jnp.concatenate([x] * n, axis=0)


def _nsa_kernel(q_ref, ksel_ref, vsel_ref, kwin_ref, vwin_ref, cmp_ref, gate_ref, ovt_ref,
                tsel_ref, twin_ref, o_ref, acc_ref, *, tk, n_cmp, n_top, nt):
    qi = pl.program_id(1)
    s0 = qi * TILE
    H, G = NSA_HPG, NSA_GROUPS
    nc = cmp_ref.shape[2]
    wlen = NSA_WINDOW + TILE
    nb = wlen // TILE
    wt0 = jnp.maximum(qi - (nb - 1), 0)
    w0 = pl.multiple_of(wt0 * TILE, TILE)
    widx = [jnp.where(wt0 + b <= qi, (nb - 1) - (qi - (wt0 + b)), nb) for b in range(nb)]

    t_row = s0 + lax.broadcasted_iota(jnp.int32, (TILE, nc), 0)
    n_idx = lax.broadcasted_iota(jnp.int32, (TILE, nc), 1)
    ok = jnp.where(n_idx * NSA_CMP_STRIDE + (NSA_CMP_LEN - 1) <= t_row, 1.0, 0.0)
    ok = _stack_heads(jnp.where(n_idx < n_cmp, ok, 0.0), H)

    q_aug, o_fix = [], []
    for g in range(G):
        qs = jnp.concatenate([q_ref[0, :, LANE * (H * g + h):LANE * (H * g + h + 1)] for h in range(H)], axis=0)
        s_c = jnp.where(ok > 0, _nt(qs, cmp_ref[0, g]), NEG)
        p_c = jnp.exp2(s_c - jnp.max(s_c, axis=-1, keepdims=True)) * ok
        l_c = jnp.sum(p_c, axis=-1, keepdims=True)
        p_c = p_c * jnp.where(l_c > 0, 1.0 / l_c, 0.0)
        o_c = jnp.dot(p_c.astype(BF16), cmp_ref[0, G + g], preferred_element_type=F32)
        p_cat = jnp.concatenate([p_c[TILE * h:TILE * (h + 1)] for h in range(H)], axis=1).astype(BF16)
        imp = _nt(ovt_ref[...], p_cat)[HEAD_DIM:]
        nj = imp.shape[0]
        j = lax.broadcasted_iota(jnp.int32, (nj, TILE), 0)
        t = s0 + lax.broadcasted_iota(jnp.int32, (nj, TILE), 1)
        cur = t // NSA_SEL_LEN
        forced = jnp.where(j == 0, 1.0, 0.0) + jnp.where(j == cur, 1.0, 0.0) + jnp.where(j == cur - 1, 1.0, 0.0)
        imp = jnp.where(forced > 0, NSA_FORCE, imp)
        imp = jnp.where(j * NSA_SEL_LEN <= t, imp, NEG)
        rank = jnp.zeros((nj, TILE), F32)
        for k in range(nj):
            vk = imp[k:k + 1, :]
            rank = rank + jnp.where(j > k, jnp.where(vk >= imp, 1.0, 0.0), jnp.where(vk > imp, 1.0, 0.0))
        mneg = jnp.where(rank < n_top, 0.0, NEG)
        m_t = jnp.concatenate([jnp.zeros((LANE - nj, TILE), F32), mneg], axis=0)
        q_aug.append(qs + _stack_heads(m_t.T.astype(BF16), H))
        s_w = _nt(qs, kwin_ref[0, pl.ds(w0, wlen), LANE * g:LANE * (g + 1)])
        s_w = s_w + jnp.concatenate(
            [jnp.concatenate([twin_ref[H * g + h, widx[b]] for b in range(nb)], axis=1) for h in range(H)], axis=0)
        p_w = jnp.exp2(s_w - jnp.max(s_w, axis=-1, keepdims=True))
        o_w = jnp.dot(p_w.astype(BF16), vwin_ref[0, pl.ds(w0, wlen), LANE * g:LANE * (g + 1)],
                      preferred_element_type=F32)
        o_w = o_w / o_w[:, HEAD_DIM:HEAD_DIM + 1]
        gt = gate_ref[0, :, LANE * g:LANE * (g + 1)]
        gate = lambda br: jnp.concatenate([gt[:, 3 * h + br:3 * h + br + 1] for h in range(H)], axis=0)
        o_fix.append((gate(0) * o_c + gate(2) * o_w, gate(1)))

    sub = tk // TILE

    def body(kc, carry):
        k0 = pl.multiple_of(kc * tk, tk)
        idx = [jnp.minimum(jnp.maximum(qi - (kc * sub + c) + 1, 0), nt - 1) for c in range(sub)]
        out = []
        for g in range(G):
            m = carry[g]
            s = _nt(q_aug[g], ksel_ref[0, pl.ds(k0, tk), LANE * g:LANE * (g + 1)])
            s = s + jnp.concatenate(
                [jnp.concatenate([tsel_ref[H * g + h, idx[c]] for c in range(sub)], axis=1) for h in range(H)], axis=0)
            m_new = jnp.maximum(m, jnp.max(s, axis=-1, keepdims=True))
            alpha = jnp.exp2(m - m_new)
            p = jnp.exp2(s - m_new)
            acc_ref[g] = acc_ref[g] * alpha + jnp.dot(
                p.astype(BF16), vsel_ref[0, pl.ds(k0, tk), LANE * g:LANE * (g + 1)], preferred_element_type=F32)
            out.append(m_new)
        return tuple(out)

    acc_ref[...] = jnp.zeros_like(acc_ref)
    lax.fori_loop(0, qi // sub + 1, body, (jnp.full((H * TILE, 1), NEG, F32),) * G)
    outs = []
    for g in range(G):
        acc = acc_ref[g]
        o = o_fix[g][0] + o_fix[g][1] * (acc / acc[:, HEAD_DIM:HEAD_DIM + 1])
        outs += [o[TILE * h:TILE * (h + 1), :HEAD_DIM] for h in range(H)]
    o_ref[0] = jnp.concatenate(outs, axis=1).astype(BF16)


def _nsa_attention(q, ksel, vsel, kwin, vwin, cmp, gates, ovt, tsel, twin, tk):
    B, S, _ = q.shape
    n_q = S // TILE
    full = lambda a: pl.BlockSpec((1,) + a.shape[1:], lambda b, i: (b,) + (0,) * (a.ndim - 1))
    return pl.pallas_call(
        functools.partial(_nsa_kernel, tk=tk, n_cmp=S // NSA_CMP_STRIDE - 1,
                          n_top=min(NSA_TOP_N, S // NSA_SEL_LEN), nt=tsel.shape[1]),
        grid=(B, n_q),
        in_specs=[pl.BlockSpec((1, TILE, q.shape[2]), lambda b, i: (b, i, 0)),
                  full(ksel), full(vsel), full(kwin), full(vwin), full(cmp),
                  pl.BlockSpec((1, TILE, gates.shape[2]), lambda b, i: (b, i, 0)),
                  _resident(ovt.shape, lambda b, i: (0, 0)),
                  _resident(tsel.shape, lambda b, i: (0, 0, 0, 0)),
                  _resident(twin.shape, lambda b, i: (0, 0, 0, 0))],
        out_specs=pl.BlockSpec((1, TILE, HALF), lambda b, i: (b, i, 0)),
        out_shape=jax.ShapeDtypeStruct((B, S, HALF), BF16),
        scratch_shapes=[pltpu.VMEM((NSA_GROUPS, NSA_HPG * TILE, LANE), F32)],
        compiler_params=_cparams(2),
        name="nsa_attention",
    )(q, ksel, vsel, kwin, vwin, cmp, gates, ovt, tsel, twin)


def _overlap_t(S):
    n_cmp = (S - NSA_CMP_LEN) // NSA_CMP_STRIDE + 1
    n_sel = S // NSA_SEL_LEN
    cmp_idx = np.arange(n_cmp)[:, None] * NSA_CMP_STRIDE + np.arange(NSA_CMP_LEN)[None, :]
    ov = (cmp_idx[:, :, None] // NSA_SEL_LEN == np.arange(n_sel)[None, None, :]).mean(axis=1)
    out = np.zeros((LANE, S // NSA_CMP_STRIDE), np.float32)
    out[HEAD_DIM:HEAD_DIM + n_sel, :n_cmp] = ov.T
    return jnp.asarray(np.tile(out, (1, NSA_HPG)), BF16)


def _conformer_kernel(x_ref, w_ref, b_ref, g_ref, beta_ref, o_ref, buf, xs, *, tm):
    halo = 32

    @pl.when(pl.program_id(1) == 0)
    def _():
        buf[0:halo] = jnp.zeros((halo, buf.shape[1]), F32)

    buf[halo:halo + tm] = x_ref[0]
    n = tm + halo - 8
    for ph in range(1, 8):
        xs[ph, 0:n] = buf[pl.ds(ph, n), :]
    acc = jnp.zeros((tm, buf.shape[1]), F32) + b_ref[...]
    for j in range(CONV_WIDTH):
        off = halo - (CONV_WIDTH - 1) + j
        a, ph = off // 8, off % 8
        tap = buf[pl.ds(8 * a, tm), :] if ph == 0 else xs[ph, pl.ds(8 * a, tm), :]
        acc = acc + w_ref[j:j + 1, :] * tap
    mu = jnp.mean(acc, axis=-1, keepdims=True)
    var = jnp.mean(jnp.square(acc - mu), axis=-1, keepdims=True)
    y = (acc - mu) * lax.rsqrt(var + EPS) * g_ref[...] + beta_ref[...]
    o_ref[0] = (y * jax.nn.sigmoid(y)).astype(BF16)
    buf[0:halo] = buf[tm:tm + halo]


def _conformer(glu, w, b, g, beta, tm):
    B, S, C = glu.shape
    vec = lambda r: pl.BlockSpec((r, C), lambda bb, i: (0, 0))
    return pl.pallas_call(
        functools.partial(_conformer_kernel, tm=tm),
        grid=(B, S // tm),
        in_specs=[pl.BlockSpec((1, tm, C), lambda bb, i: (bb, i, 0)), vec(CONV_WIDTH), vec(1), vec(1), vec(1)],
        out_specs=pl.BlockSpec((1, tm, C), lambda bb, i: (bb, i, 0)),
        out_shape=jax.ShapeDtypeStruct((B, S, C), BF16),
        scratch_shapes=[pltpu.VMEM((tm + 32, C), F32), pltpu.VMEM((8, tm + 32, C), F32)],
        compiler_params=_cparams(2),
        name="conformer_conv",
    )(glu, w, b, g, beta)


def _even_out_kernel(x_ref, a_ref, u_ref, w_ref, o_ref):
    o_ref[0] = (x_ref[0] + jnp.dot(a_ref[0], w_ref[:HALF, :], preferred_element_type=F32)
                + jnp.dot(u_ref[0], w_ref[HALF:, :], preferred_element_type=F32))


def _even_out(x, a, u, w, tm):
    B, S, D = x.shape
    row = lambda n: pl.BlockSpec((1, tm, n), lambda b, i: (b, i, 0))
    return pl.pallas_call(
        _even_out_kernel, grid=(B, S // tm),
        in_specs=[row(D), row(HALF), row(HALF), _resident((D, D), lambda b, i: (0, 0))],
        out_specs=row(D), out_shape=jax.ShapeDtypeStruct((B, S, D), F32),
        compiler_params=_cparams(2), name="even_out",
    )(x, a, u, w)


def _odd_out_kernel(x_ref, c_ref, o1_ref, o4_ref, o16_ref, l1_ref, l4_ref, l16_ref, w_ref, o_ref,
                    b_o4, b_o16, b_l4, b_l16, *, tm):
    for src, dst, d in ((o4_ref, b_o4, 4), (l4_ref, b_l4, 4), (o16_ref, b_o16, 16), (l16_ref, b_l16, 16)):
        for r in range(d):
            blk = src[0, r].astype(F32)
            for c in range(HALF // LANE):
                dst[c, pl.ds(r, tm // d, stride=d), :] = blk[:, LANE * c:LANE * (c + 1)]
    full = lambda buf: jnp.concatenate([buf[c] for c in range(HALF // LANE)], axis=1)
    l1, l2, l3 = l1_ref[0, 0], full(b_l4), full(b_l16)
    m = jnp.maximum(jnp.maximum(l1, l2), l3)
    e1, e2, e3 = jnp.exp2(l1 - m), jnp.exp2(l2 - m), jnp.exp2(l3 - m)
    od = (e1 * o1_ref[0, 0] + e2 * full(b_o4) + e3 * full(b_o16)) / (e1 + e2 + e3)
    o_ref[0] = (x_ref[0] + jnp.dot(c_ref[0], w_ref[:HALF, :], preferred_element_type=F32)
                + jnp.dot(od.astype(BF16), w_ref[HALF:, :], preferred_element_type=F32))


def _odd_out(x, oc, os_, ls, w, tm):
    B, S, D = x.shape
    row = lambda n: pl.BlockSpec((1, tm, n), lambda b, i: (b, i, 0))
    dil = lambda d: pl.BlockSpec((1, d, tm // d, HALF), lambda b, i: (b, 0, i, 0))
    return pl.pallas_call(
        functools.partial(_odd_out_kernel, tm=tm), grid=(B, S // tm),
        in_specs=[row(D), row(HALF), dil(1), dil(4), dil(16), dil(1), dil(4), dil(16),
                  _resident((D, D), lambda b, i: (0, 0))],
        out_specs=row(D), out_shape=jax.ShapeDtypeStruct((B, S, D), F32),
        scratch_shapes=[pltpu.VMEM((HALF // LANE, tm, LANE), F32)] * 4,
        compiler_params=_cparams(2), name="odd_out",
    )(x, oc, *os_, *ls, w)


def _odd_proj_kernel(x_ref, g_ref, w_ref, cq_ref, ck_ref, cv_ref, q1_ref, k1_ref, v1_ref,
                     q4_ref, k4_ref, v4_ref, q16_ref, k16_ref, v16_ref, ybuf, *, tm):
    h = _rms_bf16(x_ref[0], g_ref[...])
    scale = Q_SCALE
    for n, (ref, s) in enumerate(((cq_ref, scale), (ck_ref, 1.0), (cv_ref, 1.0))):
        y = jnp.dot(h, w_ref[:, HALF * n:HALF * (n + 1)], preferred_element_type=F32)
        ref[0] = (y * s).astype(BF16)
    for n, (r1, r4, r16, s) in enumerate(((q1_ref, q4_ref, q16_ref, scale), (k1_ref, k4_ref, k16_ref, 1.0),
                                          (v1_ref, v4_ref, v16_ref, 1.0))):
        y = jnp.dot(h, w_ref[:, HALF * (n + 3):HALF * (n + 4)], preferred_element_type=F32) * s
        r1[0, 0] = y.astype(BF16)
        for c in range(HALF // LANE):
            ybuf[c] = y[:, LANE * c:LANE * (c + 1)]
        for d, ref in ((4, r4), (16, r16)):
            for r in range(d):
                ref[0, r] = jnp.concatenate([ybuf[c, pl.ds(r, tm // d, stride=d), :] for c in range(HALF // LANE)],
                                            axis=1).astype(BF16)


def _odd_proj(x, g, w, tm):
    B, S, D = x.shape
    row = lambda n: pl.BlockSpec((1, tm, n), lambda b, i: (b, i, 0))
    dil = lambda d: pl.BlockSpec((1, d, tm // d, HALF), lambda b, i: (b, 0, i, 0))
    dshape = lambda d: jax.ShapeDtypeStruct((B, d, S // d, HALF), BF16)
    return pl.pallas_call(
        functools.partial(_odd_proj_kernel, tm=tm), grid=(B, S // tm),
        in_specs=[row(D), _resident((1, D), lambda b, i: (0, 0)), _resident((D, 6 * HALF), lambda b, i: (0, 0))],
        out_specs=[row(HALF)] * 3 + [dil(1)] * 3 + [dil(4)] * 3 + [dil(16)] * 3,
        out_shape=[jax.ShapeDtypeStruct((B, S, HALF), BF16)] * 3 + [dshape(1)] * 3 + [dshape(4)] * 3 + [dshape(16)] * 3,
        scratch_shapes=[pltpu.VMEM((HALF // LANE, tm, LANE), F32)],
        compiler_params=_cparams(2), name="odd_proj",
    )(x, g, w)


def _diff_kernel(lp_ref, q_ref, k_ref, v_ref, t_ref, g_ref, o_ref, acc_ref, *, tk, lam_init, nt, nq):
    qi0 = pl.program_id(1) * nq
    H = DIFF_HEADS
    lp = lp_ref[...]
    lam = (jnp.exp(jnp.sum(lp[0:1] * lp[1:2], axis=-1, keepdims=True))
           - jnp.exp(jnp.sum(lp[2:3] * lp[3:4], axis=-1, keepdims=True)) + lam_init)
    lane = lax.broadcasted_iota(jnp.int32, (TILE, LANE), 1)
    qs = []
    for h in range(H):
        parts = []
        for u in range(nq):
            q = q_ref[0, pl.ds(u * TILE, TILE), LANE * h:LANE * (h + 1)]
            zero = jnp.zeros_like(q)
            parts += [jnp.where(lane < HEAD_DIM, q, zero), jnp.where(lane >= HEAD_DIM, q, zero)]
        qs.append(jnp.concatenate(parts, axis=0))
    sub = tk // TILE
    ones_col = jnp.where(lax.broadcasted_iota(jnp.int32, (tk, LANE), 1) == 0, 1.0, 0.0).astype(BF16)

    def body(kc, carry):
        k0 = pl.multiple_of(kc * tk, tk)
        out = []
        for h in range(H):
            m = carry[h]
            s = _nt(qs[h], k_ref[0, pl.ds(k0, tk), LANE * h:LANE * (h + 1)])
            rows = []
            for u in range(nq):
                bt = jnp.concatenate([t_ref[h, jnp.minimum(jnp.maximum(qi0 + u - (kc * sub + c) + 1, 0), nt - 1)]
                                      for c in range(sub)], axis=1)
                rows += [bt, bt]
            s = s + jnp.concatenate(rows, axis=0)
            m_new = jnp.maximum(m, jnp.max(s, axis=-1, keepdims=True))
            alpha = jnp.exp2(m - m_new)
            p = jnp.exp2(s - m_new)
            v_aug = jnp.concatenate([v_ref[0, pl.ds(k0, tk), LANE * h:LANE * (h + 1)], ones_col], axis=1)
            acc_ref[h] = acc_ref[h] * alpha + jnp.dot(p.astype(BF16), v_aug, preferred_element_type=F32)
            out.append(m_new)
        return tuple(out)

    acc_ref[...] = jnp.zeros_like(acc_ref)
    lax.fori_loop(0, (qi0 + nq - 1) // sub + 1, body, (jnp.full((2 * nq * TILE, 1), NEG, F32),) * H)
    for u in range(nq):
        outs = []
        for h in range(H):
            acc = acc_ref[h, pl.ds(2 * u * TILE, 2 * TILE), :]
            on = acc[:, :LANE] / acc[:, LANE:LANE + 1]
            o = on[:TILE] - lam * on[TILE:]
            o = o * lax.rsqrt(jnp.mean(o * o, axis=-1, keepdims=True) + EPS) * g_ref[...] * (1.0 - lam_init)
            outs.append(o.astype(BF16))
        o_ref[0, pl.ds(u * TILE, TILE), :] = jnp.concatenate(outs, axis=1)


def _diff_attention(lp, q, k, v, tiles, g, lam_init, tk, nq):
    B, S, _ = q.shape
    return pl.pallas_call(
        functools.partial(_diff_kernel, tk=tk, lam_init=lam_init, nt=tiles.shape[1], nq=nq),
        grid=(B, S // (nq * TILE)),
        in_specs=[pl.BlockSpec(lp.shape, lambda b, i: (0, 0)),
                  pl.BlockSpec((1, nq * TILE, HALF), lambda b, i: (b, i, 0)),
                  pl.BlockSpec((1, S, HALF), lambda b, i: (b, 0, 0)),
                  pl.BlockSpec((1, S, HALF), lambda b, i: (b, 0, 0)),
                  _resident(tiles.shape, lambda b, i: (0, 0, 0, 0)),
                  pl.BlockSpec(g.shape, lambda b, i: (0, 0))],
        out_specs=pl.BlockSpec((1, nq * TILE, HALF), lambda b, i: (b, i, 0)),
        out_shape=jax.ShapeDtypeStruct((B, S, HALF), BF16),
        scratch_shapes=[pltpu.VMEM((DIFF_HEADS, 2 * nq * TILE, 2 * LANE), F32)],
        compiler_params=_cparams(2), name="diff_attention",
    )(lp, q, k, v, tiles, g)


def _dilated_kernel(q_ref, k_ref, v_ref, t_ref, o_ref, lse_ref, *, nqb):
    n0 = pl.program_id(2) * nqb
    H = 4
    W = H * HEAD_DIM
    lane = lax.broadcasted_iota(jnp.int32, (TILE, W), 1) // HEAD_DIM
    col = lax.broadcasted_iota(jnp.int32, (1, 2 * TILE), 1)
    for qb in range(nqb):
        n = n0 + qb
        cur = pl.multiple_of(n * TILE, TILE)
        prev = pl.multiple_of(jnp.maximum(n - 1, 0) * TILE, TILE)
        pmask = jnp.where((col < TILE) & (n == 0), NEG, 0.0)
        outs, lses = [], []
        for c in range(2):
            cs = slice(W * c, W * (c + 1))
            q = q_ref[0, 0, pl.ds(qb * TILE, TILE), cs]
            zero = jnp.zeros_like(q)
            qs = jnp.concatenate([jnp.where(lane == h, q, zero) for h in range(H)], axis=0)
            kcat = jnp.concatenate([k_ref[0, 0, pl.ds(prev, TILE), cs], k_ref[0, 0, pl.ds(cur, TILE), cs]], axis=0)
            vcat = jnp.concatenate([v_ref[0, 0, pl.ds(prev, TILE), cs], v_ref[0, 0, pl.ds(cur, TILE), cs]], axis=0)
            s = _nt(qs, kcat)
            s = s + jnp.concatenate(
                [jnp.concatenate([t_ref[H * c + h, 0], t_ref[H * c + h, 1]], axis=1) for h in range(H)], axis=0)
            s = s + pmask
            m = jnp.max(s, axis=-1, keepdims=True)
            p = jnp.exp2(s - m)
            l = jnp.sum(p, axis=-1, keepdims=True)
            r = jnp.dot(p.astype(BF16), vcat, preferred_element_type=F32) / l
            lse = m + jnp.log2(l)
            out = jnp.zeros((TILE, W), F32)
            lse_out = jnp.zeros((TILE, W), F32)
            for h in range(H):
                out = jnp.where(lane == h, r[TILE * h:TILE * (h + 1)], out)
                lse_out = jnp.where(lane == h, lse[TILE * h:TILE * (h + 1)], lse_out)
            outs.append(out.astype(BF16))
            lses.append(lse_out)
        o_ref[0, 0, pl.ds(qb * TILE, TILE), :] = jnp.concatenate(outs, axis=1)
        lse_ref[0, 0, pl.ds(qb * TILE, TILE), :] = jnp.concatenate(lses, axis=1)


def _dilated_group(q, k, v, tiles, nqb):
    B, dil, L, C = q.shape
    nqb = min(nqb, L // TILE)
    qb = pl.BlockSpec((1, 1, nqb * TILE, C), lambda b, r, n: (b, r, n, 0))
    kvb = pl.BlockSpec((1, 1, L, C), lambda b, r, n: (b, r, 0, 0))
    return pl.pallas_call(
        functools.partial(_dilated_kernel, nqb=nqb),
        grid=(B, dil, L // (nqb * TILE)),
        in_specs=[qb, kvb, kvb, _resident(tiles.shape, lambda b, r, n: (0, 0, 0, 0))],
        out_specs=[qb, qb],
        out_shape=[jax.ShapeDtypeStruct((B, dil, L, C), BF16), jax.ShapeDtypeStruct((B, dil, L, C), F32)],
        compiler_params=_cparams(3), name=f"dilated_attention_d{dil}",
    )(q, k, v, tiles)


def _ffn_chunks():
    edges = list(range(0, D_FF, FF_CHUNK)) + [D_FF]
    return list(zip(edges[:-1], edges[1:]))


def _ffn_kernel(x_ref, g_ref, wg_ref, wu_ref, cw_ref, cb_ref, wd_ref, fg_ref, o_ref, gbuf, halo, *, tm, final):
    x = x_ref[0]
    h = _rms_bf16(x, g_ref[...])

    @pl.when(pl.program_id(1) == 0)
    def _():
        halo[...] = jnp.zeros_like(halo)

    chunks = _ffn_chunks()
    gate_up = lambda a, b: (jnp.dot(h, wg_ref[:, a:b], preferred_element_type=F32),
                            jnp.dot(h, wu_ref[:, a:b], preferred_element_type=F32))
    o_ref[0] = x
    nxt = gate_up(*chunks[0])
    for c, (a, b) in enumerate(chunks):
        w = b - a
        gg, uu = nxt
        if c + 1 < len(chunks):
            nxt = gate_up(*chunks[c + 1])
        gbuf[c % 2, 0:8, 0:w] = halo[c, :, 0:w]
        gbuf[c % 2, 8:8 + tm, 0:w] = gg
        halo[c, :, 0:w] = gg[tm - 8:tm]
        conv = (cw_ref[0:1, a:b] * gbuf[c % 2, pl.ds(6, tm), 0:w] + cw_ref[1:2, a:b] * gbuf[c % 2, pl.ds(7, tm), 0:w]
                + cw_ref[2:3, a:b] * gg + cb_ref[:, a:b])
        act = 0.5 * conv * (1.0 + lax.erf(conv * (2.0 ** -0.5))) * uu
        o_ref[0] += jnp.dot(act.astype(BF16), wd_ref[a:b, :], preferred_element_type=F32)
    if final:
        y = o_ref[0]
        o_ref[0] = y * lax.rsqrt(jnp.mean(y * y, axis=-1, keepdims=True) + EPS) * fg_ref[...]


def _ffn(x, g, wg, wu, cw, cb, wd, fg, tm, final):
    B, S, D = x.shape
    row = pl.BlockSpec((1, tm, D), lambda b, i: (b, i, 0))
    const = lambda shape: _resident(shape, lambda b, i: (0,) * len(shape))
    return pl.pallas_call(
        functools.partial(_ffn_kernel, tm=tm, final=final),
        grid=(B, S // tm),
        in_specs=[row, const((1, D)), const((D, D_FF)), const((D, D_FF)), const(cw.shape), const((1, D_FF)),
                  const((D_FF, D)), const((1, D))],
        out_specs=row, out_shape=jax.ShapeDtypeStruct((B, S, D), F32),
        scratch_shapes=[pltpu.VMEM((2, tm + 8, FF_CHUNK), F32), pltpu.VMEM((len(_ffn_chunks()), 8, FF_CHUNK), F32)],
        compiler_params=_cparams(2), name="conv_glu_ffn",
    )(x, g, wg, wu, cw, cb, wd, fg)


def kernel(x, rel_bias_table, mix_norm_g, ffn_norm_g, ffn_w_gate, ffn_w_up, ffn_conv_w, ffn_conv_b, ffn_w_down,
           ev_w_in, ev_w_out, nsa_phi_pos, nsa_phi_w1, nsa_phi_w2, conv_dw_w, conv_dw_b, conv_ln_g, conv_ln_b,
           od_w_in, od_w_out, diff_lambda, diff_norm_g, final_norm_g):
    B, S, D = x.shape
    depth = mix_norm_g.shape[0]
    assert D == D_MODEL and S % 2048 == 0 and S // NSA_SEL_LEN <= HEAD_DIM
    tm = 512
    tk = 512
    n_q = S // TILE
    NC = S // NSA_CMP_STRIDE

    tsel = _causal_tiles(rel_bias_table, REL_COL_NSA, NSA_GROUPS * NSA_HPG, n_q)
    twin = _window_tiles(rel_bias_table, REL_COL_NSA, NSA_GROUPS * NSA_HPG)
    tdiff = _causal_tiles(rel_bias_table, REL_COL_DIFF, DIFF_HEADS, n_q)
    tdil = [_dilated_tiles(rel_bias_table, REL_COL_DIL, DIL_HEADS, dil) for _, dil in DIL_PAIRS]
    ovt = _overlap_t(S)

    for layer in range(depth):
        g_mix = mix_norm_g[layer][None, :]
        if layer % 2 == 0:
            e = layer // 2
            q, kvc, ksel, vsel, kwin, vwin, gates, glu = _even_proj(x, g_mix, _even_proj_weights(ev_w_in[e]), tm)
            r = kvc.reshape(B, NC, NSA_CMP_STRIDE, 4, HEAD_DIM).transpose(0, 3, 1, 2, 4)
            r = r.reshape(B, 4, NC, NSA_CMP_STRIDE * HEAD_DIM)
            w2 = jnp.pad(nsa_phi_w2[e], ((0, 0), (0, 0), (0, LANE - HEAD_DIM))).astype(BF16)
            cmp = _compress(r, nsa_phi_pos[e].reshape(2, 1, NSA_CMP_LEN * HEAD_DIM), nsa_phi_w1[e].astype(BF16), w2)
            o_a = _nsa_attention(q, ksel, vsel, kwin, vwin, cmp, gates, ovt, tsel, twin, tk)
            u = _conformer(glu, conv_dw_w[e], conv_dw_b[e][None, :], conv_ln_g[e][None, :],
                           conv_ln_b[e][None, :], tm)
            x = _even_out(x, o_a, u, ev_w_out[e].astype(BF16), tm)
        else:
            o = layer // 2
            lam_init = 0.8 - 0.6 * math.exp(-0.3 * layer)
            cq, ck, cv, *dil_qkv = _odd_proj(x, g_mix, od_w_in[o].astype(BF16), tm)
            o_c = _diff_attention(diff_lambda[o], cq, ck, cv, tdiff, diff_norm_g[o][None, :], lam_init, tk, nq=2)
            outs, lses = zip(*[_dilated_group(*dil_qkv[3 * n:3 * n + 3], tdil[n], 8) for n in range(len(DIL_PAIRS))])
            x = _odd_out(x, o_c, outs, lses, od_w_out[o].astype(BF16), tm)
        x = _ffn(x, ffn_norm_g[layer][None, :], ffn_w_gate[layer].astype(BF16), ffn_w_up[layer].astype(BF16),
                 ffn_conv_w[layer], ffn_conv_b[layer][None, :], ffn_w_down[layer].astype(BF16),
                 final_norm_g[None, :], tm, final=(layer == depth - 1))
    return x
```

```python
import functools
import math

import numpy as np
import jax
import jax.numpy as jnp
from jax import lax
from jax.experimental import pallas as pl
from jax.experimental.pallas import tpu as pltpu

F32, BF16 = jnp.float32, jnp.bfloat16

D_MODEL = 1024
HEAD_DIM = 64
HALF = 512
NSA_GROUPS = 2
NSA_HPG = 4
NSA_CMP_LEN = 32
NSA_CMP_STRIDE = 16
NSA_SEL_LEN = 64
NSA_TOP_N = 16
NSA_WINDOW = 512
NSA_FORCE = 1e6
CONV_WIDTH = 31
DIFF_HEADS = 4
DIL_HEADS = 8
DIL_PAIRS = ((128, 1), (512, 4), (2048, 16))
D_FF = 2816
REL_BUCKETS = 32
REL_MAX_DIST = 2048
REL_COL_NSA = 0
REL_COL_DIFF = 8
REL_COL_DIL = 12
EPS = 1e-6
NEG = -1e30
LOG2E = 1.4426950408889634
Q_SCALE = HEAD_DIM ** -0.5 * LOG2E

LANE = 128
TILE = 128
FF_CHUNK = 1024
VMEM_LIMIT = 56 * 1024 * 1024

_Q0, _KVC0, _KSEL0, _VSEL0, _KWIN0, _VWIN0, _GATE0, _GLUA0, _GLUG0, _EV_COLS = (
    0, 1024, 1280, 1536, 1792, 2048, 2304, 2560, 3072, 3584)


def _cparams(n_axes):
    return pltpu.CompilerParams(dimension_semantics=("arbitrary",) * n_axes,
                                vmem_limit_bytes=VMEM_LIMIT)


def _resident(shape, index_map):
    return pl.BlockSpec(shape, index_map, pipeline_mode=pl.Buffered(1))


def _nt(a, b):
    return lax.dot_general(a, b, (((1,), (1,)), ((), ())), preferred_element_type=F32)


def _rms_bf16(x, g):
    ms = jnp.mean(x * x, axis=-1, keepdims=True)
    return (x * lax.rsqrt(ms + EPS) * g).astype(BF16)


def _rel_bucket(dist):
    n = jnp.maximum(dist, 0)
    max_exact = REL_BUCKETS // 2
    nf = jnp.maximum(n, 1).astype(F32)
    large = max_exact + (jnp.log(nf / max_exact) / math.log(REL_MAX_DIST / max_exact)
                         * (REL_BUCKETS - max_exact)).astype(jnp.int32)
    large = jnp.minimum(large, REL_BUCKETS - 1)
    return jnp.where(n < max_exact, n, large)


def _bias_tiles(table, col0, n_heads, dist, valid):
    bucket = _rel_bucket(jnp.asarray(dist, jnp.int32))
    onehot = (bucket[..., None] == jnp.arange(REL_BUCKETS, dtype=jnp.int32)).astype(F32)
    vals = jnp.einsum("tijb,bh->htij", onehot, table[:, col0:col0 + n_heads].astype(F32),
                      precision=lax.Precision.HIGHEST)
    return jnp.where(jnp.asarray(valid)[None], vals * LOG2E, NEG)


def _n_causal_tiles(n_q):
    half = REL_BUCKETS // 2
    n_far = math.ceil(half * (REL_MAX_DIST / half) ** ((half - 1) / half)) + 1
    d_far = (n_far + 2 * (TILE - 1)) // TILE + 1
    far = np.arange(TILE * d_far - (TILE - 1), TILE * (d_far + 2), dtype=np.float32)
    assert np.all(half + np.floor(np.log(far / half) / math.log(REL_MAX_DIST / half) * half) >= REL_BUCKETS - 1)
    return min(n_q, d_far + 1) + 1


def _causal_tiles(table, col0, n_heads, n_q):
    i = np.arange(TILE)[:, None] - np.arange(TILE)[None, :]
    dist = np.stack([np.full((TILE, TILE), -1)] + [TILE * d + i for d in range(_n_causal_tiles(n_q) - 1)])
    return _bias_tiles(table, col0, n_heads, dist, dist >= 0)


def _window_tiles(table, col0, n_heads):
    i = np.arange(TILE)[:, None] - np.arange(TILE)[None, :]
    dist = np.stack([NSA_WINDOW - TILE * b + i for b in range(NSA_WINDOW // TILE + 1)] + [np.full((TILE, TILE), -1)])
    return _bias_tiles(table, col0, n_heads, dist, (dist >= 0) & (dist < NSA_WINDOW))


def _dilated_tiles(table, col0, n_heads, dil):
    i = np.arange(TILE)[:, None] - np.arange(TILE)[None, :]
    sub = np.stack([TILE + i, i])
    return _bias_tiles(table, col0, n_heads, sub * dil, (sub >= 0) & (sub <= TILE))


def _even_proj_kernel(x_ref, g_ref, w_ref, q_ref, kvc_ref, ksel_ref, vsel_ref, kwin_ref, vwin_ref,
                      gate_ref, glu_ref, *, tm):
    h = _rms_bf16(x_ref[0], g_ref[...])

    def mm(a, b):
        return jnp.dot(h, w_ref[:, a:b], preferred_element_type=F32)

    q_ref[0] = (mm(_Q0, _KVC0) * Q_SCALE).astype(BF16)
    kvc_ref[0] = mm(_KVC0, _KSEL0)
    ks = mm(_KSEL0, _VSEL0)
    lane = lax.broadcasted_iota(jnp.int32, ks.shape, 1) % LANE
    pos = pl.program_id(1) * tm + lax.broadcasted_iota(jnp.int32, ks.shape, 0)
    onehot = jnp.where(lane - HEAD_DIM == pos // NSA_SEL_LEN, 1.0, 0.0)
    ksel_ref[0] = jnp.where(lane >= HEAD_DIM, onehot, ks).astype(BF16)
    vsel_ref[0] = jnp.where(lane == HEAD_DIM, 1.0, mm(_VSEL0, _KWIN0)).astype(BF16)
    kwin_ref[0] = mm(_KWIN0, _VWIN0).astype(BF16)
    vwin_ref[0] = jnp.where(lane == HEAD_DIM, 1.0, mm(_VWIN0, _GATE0)).astype(BF16)
    gate_ref[0] = jax.nn.sigmoid(mm(_GATE0, _GLUA0))
    glu_ref[0] = mm(_GLUA0, _GLUG0) * jax.nn.sigmoid(mm(_GLUG0, _EV_COLS))


def _even_proj(x, g, w, tm):
    B, S, D = x.shape
    row = lambda n: pl.BlockSpec((1, tm, n), lambda b, i: (b, i, 0))
    outs = [(1024, BF16), (256, F32), (256, BF16), (256, BF16), (256, BF16), (256, BF16), (256, F32), (512, F32)]
    return pl.pallas_call(
        functools.partial(_even_proj_kernel, tm=tm),
        grid=(B, S // tm),
        in_specs=[row(D), _resident((1, D), lambda b, i: (0, 0)), _resident((D, _EV_COLS), lambda b, i: (0, 0))],
        out_specs=[row(n) for n, _ in outs],
        out_shape=[jax.ShapeDtypeStruct((B, S, n), dt) for n, dt in outs],
        compiler_params=_cparams(2),
        name="even_proj",
    )(x, g, w)


def _even_proj_weights(w_in):
    D = w_in.shape[0]

    def slots(cols, n, width):
        p = cols.reshape(D, n, width)
        return jnp.pad(p, ((0, 0), (0, 0), (0, LANE - width))).reshape(D, n * LANE)

    kv0 = HALF
    parts = [slots(w_in[:, :HALF], 8, HEAD_DIM),
             w_in[:, kv0:kv0 + 256]]
    for br in (1, 2):
        for e in (0, 1):
            c0 = kv0 + br * 256 + e * 128
            parts.append(slots(w_in[:, c0:c0 + 128], 2, HEAD_DIM))
    g0 = kv0 + 768
    parts.append(slots(w_in[:, g0:g0 + 24], 2, 12))
    parts.append(w_in[:, g0 + 24:])
    return jnp.concatenate(parts, axis=1).astype(BF16)


def _compress_kernel(r_ref, pos_ref, w1_ref, w2_ref, o_ref):
    r = r_ref[0, 0]
    nc = r.shape[0]
    half = NSA_CMP_STRIDE * HEAD_DIM
    pos = pos_ref[0]
    lo = (r + pos[:, :half]).astype(BF16)
    hi = (pltpu.roll(r, nc - 1, 0) + pos[:, half:]).astype(BF16)
    hid = (jnp.dot(lo, w1_ref[0, :half, :], preferred_element_type=F32)
           + jnp.dot(hi, w1_ref[0, half:, :], preferred_element_type=F32))
    hid = hid * jax.nn.sigmoid(hid)
    o_ref[0, 0] = jnp.dot(hid.astype(BF16), w2_ref[0], preferred_element_type=F32).astype(BF16)


def _compress(r, pos, w1, w2):
    B, _, NC, W = r.shape
    return pl.pallas_call(
        _compress_kernel,
        grid=(B, 4),
        in_specs=[pl.BlockSpec((1, 1, NC, W), lambda b, c: (b, c, 0, 0)),
                  pl.BlockSpec((1, 1, 2 * W), lambda b, c: (c // 2, 0, 0)),
                  pl.BlockSpec((1, 2 * W, 256), lambda b, c: (c // 2, 0, 0)),
                  pl.BlockSpec((1, 256, LANE), lambda b, c: (c // 2, 0, 0))],
        out_specs=pl.BlockSpec((1, 1, NC, LANE), lambda b, c: (b, c, 0, 0)),
        out_shape=jax.ShapeDtypeStruct((B, 4, NC, LANE), BF16),
        compiler_params=_cparams(2),
        name="nsa_compress",
    )(r, pos, w1, w2)


def _stack_heads(x, n):
    return jnp.concatenate([x] * n, axis=0)


def _nsa_kernel(q_ref, ksel_ref, vsel_ref, kwin_ref, vwin_ref, cmp_ref, gate_ref, ovt_ref,
                tsel_ref, twin_ref, o_ref, acc_ref, *, tk, n_cmp, n_top, nt):
    qi = pl.program_id(1)
    s0 = qi * TILE
    H, G = NSA_HPG, NSA_GROUPS
    nc = cmp_ref.shape[2]
    wlen = NSA_WINDOW + TILE
    nb = wlen // TILE
    wt0 = jnp.maximum(qi - (nb - 1), 0)
    w0 = pl.multiple_of(wt0 * TILE, TILE)
    widx = [jnp.where(wt0 + b <= qi, (nb - 1) - (qi - (wt0 + b)), nb) for b in range(nb)]

    t_row = s0 + lax.broadcasted_iota(jnp.int32, (TILE, nc), 0)
    n_idx = lax.broadcasted_iota(jnp.int32, (TILE, nc), 1)
    ok = jnp.where(n_idx * NSA_CMP_STRIDE + (NSA_CMP_LEN - 1) <= t_row, 1.0, 0.0)
    ok = _stack_heads(jnp.where(n_idx < n_cmp, ok, 0.0), H)

    q_aug, o_fix = [], []
    for g in range(G):
        qs = jnp.concatenate([q_ref[0, :, LANE * (H * g + h):LANE * (H * g + h + 1)] for h in range(H)], axis=0)
        s_c = jnp.where(ok > 0, _nt(qs, cmp_ref[0, g]), NEG)
        p_c = jnp.exp2(s_c - jnp.max(s_c, axis=-1, keepdims=True)) * ok
        l_c = jnp.sum(p_c, axis=-1, keepdims=True)
        p_c = p_c * jnp.where(l_c > 0, 1.0 / l_c, 0.0)
        o_c = jnp.dot(p_c.astype(BF16), cmp_ref[0, G + g], preferred_element_type=F32)
        p_cat = jnp.concatenate([p_c[TILE * h:TILE * (h + 1)] for h in range(H)], axis=1).astype(BF16)
        imp = _nt(ovt_ref[...], p_cat)[HEAD_DIM:]
        nj = imp.shape[0]
        j = lax.broadcasted_iota(jnp.int32, (nj, TILE), 0)
        t = s0 + lax.broadcasted_iota(jnp.int32, (nj, TILE), 1)
        cur = t // NSA_SEL_LEN
        forced = jnp.where(j == 0, 1.0, 0.0) + jnp.where(j == cur, 1.0, 0.0) + jnp.where(j == cur - 1, 1.0, 0.0)
        imp = jnp.where(forced > 0, NSA_FORCE, imp)
        imp = jnp.where(j * NSA_SEL_LEN <= t, imp, NEG)
        rank = jnp.zeros((nj, TILE), F32)
        for k in range(nj):
            vk = imp[k:k + 1, :]
            rank = rank + jnp.where(j > k, jnp.where(vk >= imp, 1.0, 0.0), jnp.where(vk > imp, 1.0, 0.0))
        mneg = jnp.where(rank < n_top, 0.0, NEG)
        m_t = jnp.concatenate([jnp.zeros((LANE - nj, TILE), F32), mneg], axis=0)
        q_aug.append(qs + _stack_heads(m_t.T.astype(BF16), H))
        s_w = _nt(qs, kwin_ref[0, pl.ds(w0, wlen), LANE * g:LANE * (g + 1)])
        s_w = s_w + jnp.concatenate(
            [jnp.concatenate([twin_ref[H * g + h, widx[b]] for b in range(nb)], axis=1) for h in range(H)], axis=0)
        p_w = jnp.exp2(s_w - jnp.max(s_w, axis=-1, keepdims=True))
        o_w = jnp.dot(p_w.astype(BF16), vwin_ref[0, pl.ds(w0, wlen), LANE * g:LANE * (g + 1)],
                      preferred_element_type=F32)
        gt = gate_ref[0, :, LANE * g:LANE * (g + 1)]
        gate = lambda br: jnp.concatenate([gt[:, 3 * h + br:3 * h + br + 1] for h in range(H)], axis=0)
        o_fix.append((gate(0) * o_c + (gate(2) / o_w[:, HEAD_DIM:HEAD_DIM + 1]) * o_w, gate(1)))

    sub = tk // TILE

    def body(kc, carry):
        k0 = pl.multiple_of(kc * tk, tk)
        idx = [jnp.minimum(jnp.maximum(qi - (kc * sub + c) + 1, 0), nt - 1) for c in range(sub)]
        out = []
        for g in range(G):
            m = carry[g]
            s = _nt(q_aug[g], ksel_ref[0, pl.ds(k0, tk), LANE * g:LANE * (g + 1)])
            s = s + jnp.concatenate(
                [jnp.concatenate([tsel_ref[H * g + h, idx[c]] for c in range(sub)], axis=1) for h in range(H)], axis=0)
            m_new = jnp.maximum(m, jnp.max(s, axis=-1, keepdims=True))
            alpha = jnp.exp2(m - m_new)
            p = jnp.exp2(s - m_new)
            acc_ref[g] = acc_ref[g] * alpha + jnp.dot(
                p.astype(BF16), vsel_ref[0, pl.ds(k0, tk), LANE * g:LANE * (g + 1)], preferred_element_type=F32)
            out.append(m_new)
        return tuple(out)

    acc_ref[...] = jnp.zeros_like(acc_ref)
    lax.fori_loop(0, qi // sub + 1, body, (jnp.full((H * TILE, 1), NEG, F32),) * G)
    outs = []
    for g in range(G):
        acc = acc_ref[g]
        o = o_fix[g][0] + (o_fix[g][1] / acc[:, HEAD_DIM:HEAD_DIM + 1]) * acc
        outs += [o[TILE * h:TILE * (h + 1), :HEAD_DIM] for h in range(H)]
    o_ref[0] = jnp.concatenate(outs, axis=1).astype(BF16)


def _nsa_attention(q, ksel, vsel, kwin, vwin, cmp, gates, ovt, tsel, twin, tk):
    B, S, _ = q.shape
    n_q = S // TILE
    full = lambda a: pl.BlockSpec((1,) + a.shape[1:], lambda b, i: (b,) + (0,) * (a.ndim - 1))
    return pl.pallas_call(
        functools.partial(_nsa_kernel, tk=tk, n_cmp=S // NSA_CMP_STRIDE - 1,
                          n_top=min(NSA_TOP_N, S // NSA_SEL_LEN), nt=tsel.shape[1]),
        grid=(B, n_q),
        in_specs=[pl.BlockSpec((1, TILE, q.shape[2]), lambda b, i: (b, i, 0)),
                  full(ksel), full(vsel), full(kwin), full(vwin), full(cmp),
                  pl.BlockSpec((1, TILE, gates.shape[2]), lambda b, i: (b, i, 0)),
                  _resident(ovt.shape, lambda b, i: (0, 0)),
                  _resident(tsel.shape, lambda b, i: (0, 0, 0, 0)),
                  _resident(twin.shape, lambda b, i: (0, 0, 0, 0))],
        out_specs=pl.BlockSpec((1, TILE, HALF), lambda b, i: (b, i, 0)),
        out_shape=jax.ShapeDtypeStruct((B, S, HALF), BF16),
        scratch_shapes=[pltpu.VMEM((NSA_GROUPS, NSA_HPG * TILE, LANE), F32)],
        compiler_params=_cparams(2),
        name="nsa_attention",
    )(q, ksel, vsel, kwin, vwin, cmp, gates, ovt, tsel, twin)


def _overlap_t(S):
    n_cmp = (S - NSA_CMP_LEN) // NSA_CMP_STRIDE + 1
    n_sel = S // NSA_SEL_LEN
    cmp_idx = np.arange(n_cmp)[:, None] * NSA_CMP_STRIDE + np.arange(NSA_CMP_LEN)[None, :]
    ov = (cmp_idx[:, :, None] // NSA_SEL_LEN == np.arange(n_sel)[None, None, :]).mean(axis=1)
    out = np.zeros((LANE, S // NSA_CMP_STRIDE), np.float32)
    out[HEAD_DIM:HEAD_DIM + n_sel, :n_cmp] = ov.T
    return jnp.asarray(np.tile(out, (1, NSA_HPG)), BF16)


def _conformer_kernel(x_ref, w_ref, b_ref, g_ref, beta_ref, o_ref, buf, xs, *, tm):
    halo = 32

    @pl.when(pl.program_id(1) == 0)
    def _():
        buf[0:halo] = jnp.zeros((halo, buf.shape[1]), F32)

    buf[halo:halo + tm] = x_ref[0]
    n = tm + halo - 8
    for ph in range(1, 8):
        xs[ph, 0:n] = buf[pl.ds(ph, n), :]
    acc = jnp.zeros((tm, buf.shape[1]), F32) + b_ref[...]
    for j in range(CONV_WIDTH):
        off = halo - (CONV_WIDTH - 1) + j
        a, ph = off // 8, off % 8
        tap = buf[pl.ds(8 * a, tm), :] if ph == 0 else xs[ph, pl.ds(8 * a, tm), :]
        acc = acc + w_ref[j:j + 1, :] * tap
    mu = jnp.mean(acc, axis=-1, keepdims=True)
    var = jnp.mean(jnp.square(acc - mu), axis=-1, keepdims=True)
    y = (acc - mu) * lax.rsqrt(var + EPS) * g_ref[...] + beta_ref[...]
    o_ref[0] = (y * jax.nn.sigmoid(y)).astype(BF16)
    buf[0:halo] = buf[tm:tm + halo]


def _conformer(glu, w, b, g, beta, tm):
    B, S, C = glu.shape
    vec = lambda r: pl.BlockSpec((r, C), lambda bb, i: (0, 0))
    return pl.pallas_call(
        functools.partial(_conformer_kernel, tm=tm),
        grid=(B, S // tm),
        in_specs=[pl.BlockSpec((1, tm, C), lambda bb, i: (bb, i, 0)), vec(CONV_WIDTH), vec(1), vec(1), vec(1)],
        out_specs=pl.BlockSpec((1, tm, C), lambda bb, i: (bb, i, 0)),
        out_shape=jax.ShapeDtypeStruct((B, S, C), BF16),
        scratch_shapes=[pltpu.VMEM((tm + 32, C), F32), pltpu.VMEM((8, tm + 32, C), F32)],
        compiler_params=_cparams(2),
        name="conformer_conv",
    )(glu, w, b, g, beta)


def _even_out_kernel(x_ref, a_ref, u_ref, w_ref, o_ref):
    o_ref[0] = (x_ref[0] + jnp.dot(a_ref[0], w_ref[:HALF, :], preferred_element_type=F32)
                + jnp.dot(u_ref[0], w_ref[HALF:, :], preferred_element_type=F32))


def _even_out(x, a, u, w, tm):
    B, S, D = x.shape
    row = lambda n: pl.BlockSpec((1, tm, n), lambda b, i: (b, i, 0))
    return pl.pallas_call(
        _even_out_kernel, grid=(B, S // tm),
        in_specs=[row(D), row(HALF), row(HALF), _resident((D, D), lambda b, i: (0, 0))],
        out_specs=row(D), out_shape=jax.ShapeDtypeStruct((B, S, D), F32),
        compiler_params=_cparams(2), name="even_out",
    )(x, a, u, w)


def _odd_out_kernel(x_ref, c_ref, o1_ref, o4_ref, o16_ref, l1_ref, l4_ref, l16_ref, w_ref, o_ref,
                    b_o4, b_o16, b_l4, b_l16, *, tm):
    for src, dst, d in ((o4_ref, b_o4, 4), (l4_ref, b_l4, 4), (o16_ref, b_o16, 16), (l16_ref, b_l16, 16)):
        for r in range(d):
            blk = src[0, r].astype(F32)
            for c in range(HALF // LANE):
                dst[c, pl.ds(r, tm // d, stride=d), :] = blk[:, LANE * c:LANE * (c + 1)]
    full = lambda buf: jnp.concatenate([buf[c] for c in range(HALF // LANE)], axis=1)
    l1, l2, l3 = l1_ref[0, 0], full(b_l4), full(b_l16)
    m = jnp.maximum(jnp.maximum(l1, l2), l3)
    e1, e2, e3 = jnp.exp2(l1 - m), jnp.exp2(l2 - m), jnp.exp2(l3 - m)
    od = (e1 * o1_ref[0, 0] + e2 * full(b_o4) + e3 * full(b_o16)) / (e1 + e2 + e3)
    o_ref[0] = (x_ref[0] + jnp.dot(c_ref[0], w_ref[:HALF, :], preferred_element_type=F32)
                + jnp.dot(od.astype(BF16), w_ref[HALF:, :], preferred_element_type=F32))


def _odd_out(x, oc, os_, ls, w, tm):
    B, S, D = x.shape
    row = lambda n: pl.BlockSpec((1, tm, n), lambda b, i: (b, i, 0))
    dil = lambda d: pl.BlockSpec((1, d, tm // d, HALF), lambda b, i: (b, 0, i, 0))
    return pl.pallas_call(
        functools.partial(_odd_out_kernel, tm=tm), grid=(B, S // tm),
        in_specs=[row(D), row(HALF), dil(1), dil(4), dil(16), dil(1), dil(4), dil(16),
                  _resident((D, D), lambda b, i: (0, 0))],
        out_specs=row(D), out_shape=jax.ShapeDtypeStruct((B, S, D), F32),
        scratch_shapes=[pltpu.VMEM((HALF // LANE, tm, LANE), F32)] * 4,
        compiler_params=_cparams(2), name="odd_out",
    )(x, oc, *os_, *ls, w)


def _odd_proj_kernel(x_ref, g_ref, w_ref, cq_ref, ck_ref, cv_ref, q1_ref, k1_ref, v1_ref,
                     q4_ref, k4_ref, v4_ref, q16_ref, k16_ref, v16_ref, ybuf, *, tm):
    h = _rms_bf16(x_ref[0], g_ref[...])
    scale = Q_SCALE
    for n, (ref, s) in enumerate(((cq_ref, scale), (ck_ref, 1.0), (cv_ref, 1.0))):
        y = jnp.dot(h, w_ref[:, HALF * n:HALF * (n + 1)], preferred_element_type=F32)
        ref[0] = (y * s).astype(BF16)
    for n, (r1, r4, r16, s) in enumerate(((q1_ref, q4_ref, q16_ref, scale), (k1_ref, k4_ref, k16_ref, 1.0),
                                          (v1_ref, v4_ref, v16_ref, 1.0))):
        y = jnp.dot(h, w_ref[:, HALF * (n + 3):HALF * (n + 4)], preferred_element_type=F32) * s
        r1[0, 0] = y.astype(BF16)
        for c in range(HALF // LANE):
            ybuf[c] = y[:, LANE * c:LANE * (c + 1)]
        for d, ref in ((4, r4), (16, r16)):
            for r in range(d):
                ref[0, r] = jnp.concatenate([ybuf[c, pl.ds(r, tm // d, stride=d), :] for c in range(HALF // LANE)],
                                            axis=1).astype(BF16)


def _odd_proj(x, g, w, tm):
    B, S, D = x.shape
    row = lambda n: pl.BlockSpec((1, tm, n), lambda b, i: (b, i, 0))
    dil = lambda d: pl.BlockSpec((1, d, tm // d, HALF), lambda b, i: (b, 0, i, 0))
    dshape = lambda d: jax.ShapeDtypeStruct((B, d, S // d, HALF), BF16)
    return pl.pallas_call(
        functools.partial(_odd_proj_kernel, tm=tm), grid=(B, S // tm),
        in_specs=[row(D), _resident((1, D), lambda b, i: (0, 0)), _resident((D, 6 * HALF), lambda b, i: (0, 0))],
        out_specs=[row(HALF)] * 3 + [dil(1)] * 3 + [dil(4)] * 3 + [dil(16)] * 3,
        out_shape=[jax.ShapeDtypeStruct((B, S, HALF), BF16)] * 3 + [dshape(1)] * 3 + [dshape(4)] * 3 + [dshape(16)] * 3,
        scratch_shapes=[pltpu.VMEM((HALF // LANE, tm, LANE), F32)],
        compiler_params=_cparams(2), name="odd_proj",
    )(x, g, w)


def _diff_kernel(lp_ref, q_ref, k_ref, v_ref, t_ref, g_ref, o_ref, acc_ref, *, tk, lam_init, nt, nq):
    qi0 = pl.program_id(1) * nq
    H = DIFF_HEADS
    lp = lp_ref[...]
    lam = (jnp.exp(jnp.sum(lp[0:1] * lp[1:2], axis=-1, keepdims=True))
           - jnp.exp(jnp.sum(lp[2:3] * lp[3:4], axis=-1, keepdims=True)) + lam_init)
    lane = lax.broadcasted_iota(jnp.int32, (TILE, LANE), 1)
    qs = []
    for h in range(H):
        parts = []
        for u in range(nq):
            q = q_ref[0, pl.ds(u * TILE, TILE), LANE * h:LANE * (h + 1)]
            zero = jnp.zeros_like(q)
            parts += [jnp.where(lane < HEAD_DIM, q, zero), jnp.where(lane >= HEAD_DIM, q, zero)]
        qs.append(jnp.concatenate(parts, axis=0))
    sub = tk // TILE
    ones_col = jnp.where(lax.broadcasted_iota(jnp.int32, (tk, LANE), 1) == 0, 1.0, 0.0).astype(BF16)

    def body(kc, carry):
        k0 = pl.multiple_of(kc * tk, tk)
        out = []
        for h in range(H):
            m = carry[h]
            s = _nt(qs[h], k_ref[0, pl.ds(k0, tk), LANE * h:LANE * (h + 1)])
            rows = []
            for u in range(nq):
                bt = jnp.concatenate([t_ref[h, jnp.minimum(jnp.maximum(qi0 + u - (kc * sub + c) + 1, 0), nt - 1)]
                                      for c in range(sub)], axis=1)
                rows += [bt, bt]
            s = s + jnp.concatenate(rows, axis=0)
            m_new = jnp.maximum(m, jnp.max(s, axis=-1, keepdims=True))
            alpha = jnp.exp2(m - m_new)
            p = jnp.exp2(s - m_new)
            v_aug = jnp.concatenate([v_ref[0, pl.ds(k0, tk), LANE * h:LANE * (h + 1)], ones_col], axis=1)
            acc_ref[h] = acc_ref[h] * alpha + jnp.dot(p.astype(BF16), v_aug, preferred_element_type=F32)
            out.append(m_new)
        return tuple(out)

    acc_ref[...] = jnp.zeros_like(acc_ref)
    lax.fori_loop(0, (qi0 + nq - 1) // sub + 1, body, (jnp.full((2 * nq * TILE, 1), NEG, F32),) * H)
    for u in range(nq):
        outs = []
        for h in range(H):
            acc = acc_ref[h, pl.ds(2 * u * TILE, 2 * TILE), :]
            on = acc[:, :LANE] / acc[:, LANE:LANE + 1]
            o = on[:TILE] - lam * on[TILE:]
            o = o * lax.rsqrt(jnp.mean(o * o, axis=-1, keepdims=True) + EPS) * g_ref[...] * (1.0 - lam_init)
            outs.append(o.astype(BF16))
        o_ref[0, pl.ds(u * TILE, TILE), :] = jnp.concatenate(outs, axis=1)


def _diff_attention(lp, q, k, v, tiles, g, lam_init, tk, nq):
    B, S, _ = q.shape
    return pl.pallas_call(
        functools.partial(_diff_kernel, tk=tk, lam_init=lam_init, nt=tiles.shape[1], nq=nq),
        grid=(B, S // (nq * TILE)),
        in_specs=[pl.BlockSpec(lp.shape, lambda b, i: (0, 0)),
                  pl.BlockSpec((1, nq * TILE, HALF), lambda b, i: (b, i, 0)),
                  pl.BlockSpec((1, S, HALF), lambda b, i: (b, 0, 0)),
                  pl.BlockSpec((1, S, HALF), lambda b, i: (b, 0, 0)),
                  _resident(tiles.shape, lambda b, i: (0, 0, 0, 0)),
                  pl.BlockSpec(g.shape, lambda b, i: (0, 0))],
        out_specs=pl.BlockSpec((1, nq * TILE, HALF), lambda b, i: (b, i, 0)),
        out_shape=jax.ShapeDtypeStruct((B, S, HALF), BF16),
        scratch_shapes=[pltpu.VMEM((DIFF_HEADS, 2 * nq * TILE, 2 * LANE), F32)],
        compiler_params=_cparams(2), name="diff_attention",
    )(lp, q, k, v, tiles, g)


def _dilated_kernel(q_ref, k_ref, v_ref, t_ref, o_ref, lse_ref, *, nqb, nres):
    n0 = pl.program_id(2) * nqb
    H = 4
    W = H * HEAD_DIM
    lane = lax.broadcasted_iota(jnp.int32, (TILE, W), 1) // HEAD_DIM
    col = lax.broadcasted_iota(jnp.int32, (1, 2 * TILE), 1)
    for rr, qb in [(rr, qb) for rr in range(nres) for qb in range(nqb)]:
        n = n0 + qb
        cur = pl.multiple_of(n * TILE, TILE)
        prev = pl.multiple_of(jnp.maximum(n - 1, 0) * TILE, TILE)
        pmask = jnp.where((col < TILE) & (n == 0), NEG, 0.0)
        outs, lses = [], []
        for c in range(2):
            cs = slice(W * c, W * (c + 1))
            q = q_ref[0, rr, pl.ds(qb * TILE, TILE), cs]
            zero = jnp.zeros_like(q)
            qs = jnp.concatenate([jnp.where(lane == h, q, zero) for h in range(H)], axis=0)
            kcat = jnp.concatenate([k_ref[0, rr, pl.ds(prev, TILE), cs], k_ref[0, rr, pl.ds(cur, TILE), cs]], axis=0)
            vcat = jnp.concatenate([v_ref[0, rr, pl.ds(prev, TILE), cs], v_ref[0, rr, pl.ds(cur, TILE), cs]], axis=0)
            s = _nt(qs, kcat)
            s = s + jnp.concatenate(
                [jnp.concatenate([t_ref[H * c + h, 0], t_ref[H * c + h, 1]], axis=1) for h in range(H)], axis=0)
            s = s + pmask
            m = jnp.max(s, axis=-1, keepdims=True)
            p = jnp.exp2(s - m)
            l = jnp.sum(p, axis=-1, keepdims=True)
            r = jnp.dot(p.astype(BF16), vcat, preferred_element_type=F32) / l
            lse = m + jnp.log2(l)
            out = jnp.zeros((TILE, W), F32)
            lse_out = jnp.zeros((TILE, W), F32)
            for h in range(H):
                out = jnp.where(lane == h, r[TILE * h:TILE * (h + 1)], out)
                lse_out = jnp.where(lane == h, lse[TILE * h:TILE * (h + 1)], lse_out)
            outs.append(out.astype(BF16))
            lses.append(lse_out)
        o_ref[0, rr, pl.ds(qb * TILE, TILE), :] = jnp.concatenate(outs, axis=1)
        lse_ref[0, rr, pl.ds(qb * TILE, TILE), :] = jnp.concatenate(lses, axis=1)


def _dilated_group(q, k, v, tiles, n_blocks):
    B, dil, L, C = q.shape
    nqb = min(n_blocks, L // TILE)
    nres = min(n_blocks // nqb, dil)
    qb = pl.BlockSpec((1, nres, nqb * TILE, C), lambda b, r, n: (b, r, n, 0))
    kvb = pl.BlockSpec((1, nres, L, C), lambda b, r, n: (b, r, 0, 0))
    return pl.pallas_call(
        functools.partial(_dilated_kernel, nqb=nqb, nres=nres),
        grid=(B, dil // nres, L // (nqb * TILE)),
        in_specs=[qb, kvb, kvb, _resident(tiles.shape, lambda b, r, n: (0, 0, 0, 0))],
        out_specs=[qb, qb],
        out_shape=[jax.ShapeDtypeStruct((B, dil, L, C), BF16), jax.ShapeDtypeStruct((B, dil, L, C), F32)],
        compiler_params=_cparams(3), name=f"dilated_attention_d{dil}",
    )(q, k, v, tiles)


def _ffn_chunks():
    edges = list(range(0, D_FF, FF_CHUNK)) + [D_FF]
    return list(zip(edges[:-1], edges[1:]))


def _ffn_kernel(x_ref, g_ref, wg_ref, wu_ref, cw_ref, cb_ref, wd_ref, fg_ref, o_ref, gbuf, halo, *, tm, final):
    x = x_ref[0]
    h = _rms_bf16(x, g_ref[...])

    @pl.when(pl.program_id(1) == 0)
    def _():
        halo[...] = jnp.zeros_like(halo)

    chunks = _ffn_chunks()
    gate_up = lambda a, b: (jnp.dot(h, wg_ref[:, a:b], preferred_element_type=F32),
                            jnp.dot(h, wu_ref[:, a:b], preferred_element_type=F32))
    o_ref[0] = x
    nxt = gate_up(*chunks[0])
    for c, (a, b) in enumerate(chunks):
        w = b - a
        gg, uu = nxt
        if c + 1 < len(chunks):
            nxt = gate_up(*chunks[c + 1])
        gbuf[c % 2, 0:8, 0:w] = halo[c, :, 0:w]
        gbuf[c % 2, 8:8 + tm, 0:w] = gg
        halo[c, :, 0:w] = gg[tm - 8:tm]
        conv = (cw_ref[0:1, a:b] * gbuf[c % 2, pl.ds(6, tm), 0:w] + cw_ref[1:2, a:b] * gbuf[c % 2, pl.ds(7, tm), 0:w]
                + cw_ref[2:3, a:b] * gg + cb_ref[:, a:b])
        act = 0.5 * conv * (1.0 + lax.erf(conv * (2.0 ** -0.5))) * uu
        o_ref[0] += jnp.dot(act.astype(BF16), wd_ref[a:b, :], preferred_element_type=F32)
    if final:
        y = o_ref[0]
        o_ref[0] = y * lax.rsqrt(jnp.mean(y * y, axis=-1, keepdims=True) + EPS) * fg_ref[...]


def _ffn(x, g, wg, wu, cw, cb, wd, fg, tm, final):
    B, S, D = x.shape
    row = pl.BlockSpec((1, tm, D), lambda b, i: (b, i, 0))
    const = lambda shape: _resident(shape, lambda b, i: (0,) * len(shape))
    return pl.pallas_call(
        functools.partial(_ffn_kernel, tm=tm, final=final),
        grid=(B, S // tm),
        in_specs=[row, const((1, D)), const((D, D_FF)), const((D, D_FF)), const(cw.shape), const((1, D_FF)),
                  const((D_FF, D)), const((1, D))],
        out_specs=row, out_shape=jax.ShapeDtypeStruct((B, S, D), F32),
        scratch_shapes=[pltpu.VMEM((2, tm + 8, FF_CHUNK), F32), pltpu.VMEM((len(_ffn_chunks()), 8, FF_CHUNK), F32)],
        compiler_params=_cparams(2), name="conv_glu_ffn",
    )(x, g, wg, wu, cw, cb, wd, fg)


def kernel(x, rel_bias_table, mix_norm_g, ffn_norm_g, ffn_w_gate, ffn_w_up, ffn_conv_w, ffn_conv_b, ffn_w_down,
           ev_w_in, ev_w_out, nsa_phi_pos, nsa_phi_w1, nsa_phi_w2, conv_dw_w, conv_dw_b, conv_ln_g, conv_ln_b,
           od_w_in, od_w_out, diff_lambda, diff_norm_g, final_norm_g):
    B, S, D = x.shape
    depth = mix_norm_g.shape[0]
    assert D == D_MODEL and S % 2048 == 0 and S // NSA_SEL_LEN <= HEAD_DIM
    tm = 512
    tk = 512
    n_q = S // TILE
    NC = S // NSA_CMP_STRIDE

    tsel = _causal_tiles(rel_bias_table, REL_COL_NSA, NSA_GROUPS * NSA_HPG, n_q)
    twin = _window_tiles(rel_bias_table, REL_COL_NSA, NSA_GROUPS * NSA_HPG)
    tdiff = _causal_tiles(rel_bias_table, REL_COL_DIFF, DIFF_HEADS, n_q)
    tdil = [_dilated_tiles(rel_bias_table, REL_COL_DIL, DIL_HEADS, dil) for _, dil in DIL_PAIRS]
    ovt = _overlap_t(S)

    for layer in range(depth):
        g_mix = mix_norm_g[layer][None, :]
        if layer % 2 == 0:
            e = layer // 2
            q, kvc, ksel, vsel, kwin, vwin, gates, glu = _even_proj(x, g_mix, _even_proj_weights(ev_w_in[e]), tm)
            r = kvc.reshape(B, NC, NSA_CMP_STRIDE, 4, HEAD_DIM).transpose(0, 3, 1, 2, 4)
            r = r.reshape(B, 4, NC, NSA_CMP_STRIDE * HEAD_DIM)
            w2 = jnp.pad(nsa_phi_w2[e], ((0, 0), (0, 0), (0, LANE - HEAD_DIM))).astype(BF16)
            cmp = _compress(r, nsa_phi_pos[e].reshape(2, 1, NSA_CMP_LEN * HEAD_DIM), nsa_phi_w1[e].astype(BF16), w2)
            o_a = _nsa_attention(q, ksel, vsel, kwin, vwin, cmp, gates, ovt, tsel, twin, tk)
            u = _conformer(glu, conv_dw_w[e], conv_dw_b[e][None, :], conv_ln_g[e][None, :],
                           conv_ln_b[e][None, :], tm)
            x = _even_out(x, o_a, u, ev_w_out[e].astype(BF16), tm)
        else:
            o = layer // 2
            lam_init = 0.8 - 0.6 * math.exp(-0.3 * layer)
            cq, ck, cv, *dil_qkv = _odd_proj(x, g_mix, od_w_in[o].astype(BF16), tm)
            o_c = _diff_attention(diff_lambda[o], cq, ck, cv, tdiff, diff_norm_g[o][None, :], lam_init, tk, nq=2)
            outs, lses = zip(*[_dilated_group(*dil_qkv[3 * n:3 * n + 3], tdil[n], 8) for n in range(len(DIL_PAIRS))])
            x = _odd_out(x, o_c, outs, lses, od_w_out[o].astype(BF16), tm)
        x = _ffn(x, ffn_norm_g[layer][None, :], ffn_w_gate[layer].astype(BF16), ffn_w_up[layer].astype(BF16),
                 ffn_conv_w[layer], ffn_conv_b[layer][None, :], ffn_w_down[layer].astype(BF16),
                 final_norm_g[None, :], tm, final=(layer == depth - 1))
    return x
```

```python
import functools
import math

import numpy as np
import jax
import jax.numpy as jnp
from jax import lax
from jax.experimental import pallas as pl
from jax.experimental.pallas import tpu as pltpu

F32, BF16 = jnp.float32, jnp.bfloat16

D_MODEL = 1024
HEAD_DIM = 64
HALF = 512
NSA_GROUPS = 2
NSA_HPG = 4
NSA_CMP_LEN = 32
NSA_CMP_STRIDE = 16
NSA_SEL_LEN = 64
NSA_TOP_N = 16
NSA_WINDOW = 512
NSA_FORCE = 1e6
CONV_WIDTH = 31
DIFF_HEADS = 4
DIL_HEADS = 8
DIL_PAIRS = ((128, 1), (512, 4), (2048, 16))
D_FF = 2816
REL_BUCKETS = 32
REL_MAX_DIST = 2048
REL_COL_NSA = 0
REL_COL_DIFF = 8
REL_COL_DIL = 12
EPS = 1e-6
NEG = -1e30
LOG2E = 1.4426950408889634
Q_SCALE = HEAD_DIM ** -0.5 * LOG2E

LANE = 128
TILE = 128
FF_CHUNK = 1024
VMEM_LIMIT = 56 * 1024 * 1024

_Q0, _KVC0, _KSEL0, _VSEL0, _KWIN0, _VWIN0, _GATE0, _GLUA0, _GLUG0, _EV_COLS = (
    0, 1024, 1280, 1536, 1792, 2048, 2304, 2560, 3072, 3584)


def _cparams(n_axes):
    return pltpu.CompilerParams(dimension_semantics=("arbitrary",) * n_axes,
                                vmem_limit_bytes=VMEM_LIMIT)


def _resident(shape, index_map):
    return pl.BlockSpec(shape, index_map, pipeline_mode=pl.Buffered(1))


def _nt(a, b):
    return lax.dot_general(a, b, (((1,), (1,)), ((), ())), preferred_element_type=F32)


def _rms_bf16(x, g):
    ms = jnp.mean(x * x, axis=-1, keepdims=True)
    return (x * lax.rsqrt(ms + EPS) * g).astype(BF16)


def _rel_bucket(dist):
    n = jnp.maximum(dist, 0)
    max_exact = REL_BUCKETS // 2
    nf = jnp.maximum(n, 1).astype(F32)
    large = max_exact + (jnp.log(nf / max_exact) / math.log(REL_MAX_DIST / max_exact)
                         * (REL_BUCKETS - max_exact)).astype(jnp.int32)
    large = jnp.minimum(large, REL_BUCKETS - 1)
    return jnp.where(n < max_exact, n, large)


def _bias_tiles(table, col0, n_heads, dist, valid):
    bucket = _rel_bucket(jnp.asarray(dist, jnp.int32))
    onehot = (bucket[..., None] == jnp.arange(REL_BUCKETS, dtype=jnp.int32)).astype(F32)
    vals = jnp.einsum("tijb,bh->htij", onehot, table[:, col0:col0 + n_heads].astype(F32),
                      precision=lax.Precision.HIGHEST)
    return jnp.where(jnp.asarray(valid)[None], vals * LOG2E, NEG)


def _n_causal_tiles(n_q):
    half = REL_BUCKETS // 2
    n_far = math.ceil(half * (REL_MAX_DIST / half) ** ((half - 1) / half)) + 1
    d_far = (n_far + 2 * (TILE - 1)) // TILE + 1
    far = np.arange(TILE * d_far - (TILE - 1), TILE * (d_far + 2), dtype=np.float32)
    assert np.all(half + np.floor(np.log(far / half) / math.log(REL_MAX_DIST / half) * half) >= REL_BUCKETS - 1)
    return min(n_q, d_far + 1) + 1


def _causal_tiles(table, col0, n_heads, n_q):
    i = np.arange(TILE)[:, None] - np.arange(TILE)[None, :]
    dist = np.stack([np.full((TILE, TILE), -1)] + [TILE * d + i for d in range(_n_causal_tiles(n_q) - 1)])
    return _bias_tiles(table, col0, n_heads, dist, dist >= 0)


def _window_tiles(table, col0, n_heads):
    i = np.arange(TILE)[:, None] - np.arange(TILE)[None, :]
    dist = np.stack([NSA_WINDOW - TILE * b + i for b in range(NSA_WINDOW // TILE + 1)] + [np.full((TILE, TILE), -1)])
    return _bias_tiles(table, col0, n_heads, dist, (dist >= 0) & (dist < NSA_WINDOW))


def _dilated_tiles(table, col0, n_heads, dil):
    i = np.arange(TILE)[:, None] - np.arange(TILE)[None, :]
    sub = np.stack([TILE + i, i])
    return _bias_tiles(table, col0, n_heads, sub * dil, (sub >= 0) & (sub <= TILE))


def _even_proj_kernel(x_ref, g_ref, w_ref, q_ref, kvc_ref, ksel_ref, vsel_ref, kwin_ref, vwin_ref,
                      gate_ref, glu_ref, *, tm):
    h = _rms_bf16(x_ref[0], g_ref[...])

    def mm(a, b):
        return jnp.dot(h, w_ref[:, a:b], preferred_element_type=F32)

    q_ref[0] = (mm(_Q0, _KVC0) * Q_SCALE).astype(BF16)
    kvc_ref[0] = mm(_KVC0, _KSEL0)
    ks = mm(_KSEL0, _VSEL0)
    lane = lax.broadcasted_iota(jnp.int32, ks.shape, 1) % LANE
    pos = pl.program_id(1) * tm + lax.broadcasted_iota(jnp.int32, ks.shape, 0)
    onehot = jnp.where(lane - HEAD_DIM == pos // NSA_SEL_LEN, 1.0, 0.0)
    ksel_ref[0] = jnp.where(lane >= HEAD_DIM, onehot, ks).astype(BF16)
    vsel_ref[0] = jnp.where(lane == HEAD_DIM, 1.0, mm(_VSEL0, _KWIN0)).astype(BF16)
    kwin_ref[0] = mm(_KWIN0, _VWIN0).astype(BF16)
    vwin_ref[0] = jnp.where(lane == HEAD_DIM, 1.0, mm(_VWIN0, _GATE0)).astype(BF16)
    gate_ref[0] = jax.nn.sigmoid(mm(_GATE0, _GLUA0))
    glu_ref[0] = mm(_GLUA0, _GLUG0) * jax.nn.sigmoid(mm(_GLUG0, _EV_COLS))


def _even_proj(x, g, w, tm):
    B, S, D = x.shape
    row = lambda n: pl.BlockSpec((1, tm, n), lambda b, i: (b, i, 0))
    outs = [(1024, BF16), (256, F32), (256, BF16), (256, BF16), (256, BF16), (256, BF16), (256, F32), (512, F32)]
    return pl.pallas_call(
        functools.partial(_even_proj_kernel, tm=tm),
        grid=(B, S // tm),
        in_specs=[row(D), _resident((1, D), lambda b, i: (0, 0)), _resident((D, _EV_COLS), lambda b, i: (0, 0))],
        out_specs=[row(n) for n, _ in outs],
        out_shape=[jax.ShapeDtypeStruct((B, S, n), dt) for n, dt in outs],
        compiler_params=_cparams(2),
        name="even_proj",
    )(x, g, w)


def _even_proj_weights(w_in):
    D = w_in.shape[0]

    def slots(cols, n, width):
        p = cols.reshape(D, n, width)
        return jnp.pad(p, ((0, 0), (0, 0), (0, LANE - width))).reshape(D, n * LANE)

    kv0 = HALF
    parts = [slots(w_in[:, :HALF], 8, HEAD_DIM),
             w_in[:, kv0:kv0 + 256]]
    for br in (1, 2):
        for e in (0, 1):
            c0 = kv0 + br * 256 + e * 128
            parts.append(slots(w_in[:, c0:c0 + 128], 2, HEAD_DIM))
    g0 = kv0 + 768
    parts.append(slots(w_in[:, g0:g0 + 24], 2, 12))
    parts.append(w_in[:, g0 + 24:])
    return jnp.concatenate(parts, axis=1).astype(BF16)


def _compress_kernel(r_ref, pos_ref, w1_ref, w2_ref, o_ref):
    r = r_ref[0, 0]
    nc = r.shape[0]
    half = NSA_CMP_STRIDE * HEAD_DIM
    pos = pos_ref[0]
    lo = (r + pos[:, :half]).astype(BF16)
    hi = (pltpu.roll(r, nc - 1, 0) + pos[:, half:]).astype(BF16)
    hid = (jnp.dot(lo, w1_ref[0, :half, :], preferred_element_type=F32)
           + jnp.dot(hi, w1_ref[0, half:, :], preferred_element_type=F32))
    hid = hid * jax.nn.sigmoid(hid)
    o_ref[0, 0] = jnp.dot(hid.astype(BF16), w2_ref[0], preferred_element_type=F32).astype(BF16)


def _compress(r, pos, w1, w2):
    B, _, NC, W = r.shape
    return pl.pallas_call(
        _compress_kernel,
        grid=(B, 4),
        in_specs=[pl.BlockSpec((1, 1, NC, W), lambda b, c: (b, c, 0, 0)),
                  pl.BlockSpec((1, 1, 2 * W), lambda b, c: (c // 2, 0, 0)),
                  pl.BlockSpec((1, 2 * W, 256), lambda b, c: (c // 2, 0, 0)),
                  pl.BlockSpec((1, 256, LANE), lambda b, c: (c // 2, 0, 0))],
        out_specs=pl.BlockSpec((1, 1, NC, LANE), lambda b, c: (b, c, 0, 0)),
        out_shape=jax.ShapeDtypeStruct((B, 4, NC, LANE), BF16),
        compiler_params=_cparams(2),
        name="nsa_compress",
    )(r, pos, w1, w2)


def _stack_heads(x, n):
    return jnp.concatenate([x] * n, axis=0)


def _nsa_kernel(q_ref, ksel_ref, vsel_ref, kwin_ref, vwin_ref, cmp_ref, gate_ref, ovt_ref,
                tsel_ref, twin_ref, o_ref, acc_ref, *, tk, n_cmp, n_top, nt):
    qi = pl.program_id(1)
    s0 = qi * TILE
    H, G = NSA_HPG, NSA_GROUPS
    nc = cmp_ref.shape[2]
    wlen = NSA_WINDOW + TILE
    nb = wlen // TILE
    wt0 = jnp.maximum(qi - (nb - 1), 0)
    w0 = pl.multiple_of(wt0 * TILE, TILE)
    widx = [jnp.where(wt0 + b <= qi, (nb - 1) - (qi - (wt0 + b)), nb) for b in range(nb)]

    t_row = s0 + lax.broadcasted_iota(jnp.int32, (TILE, nc), 0)
    n_idx = lax.broadcasted_iota(jnp.int32, (TILE, nc), 1)
    ok = jnp.where(n_idx * NSA_CMP_STRIDE + (NSA_CMP_LEN - 1) <= t_row, 1.0, 0.0)
    ok = _stack_heads(jnp.where(n_idx < n_cmp, ok, 0.0), H)

    q_aug, o_fix = [], []
    for g in range(G):
        qs = jnp.concatenate([q_ref[0, :, LANE * (H * g + h):LANE * (H * g + h + 1)] for h in range(H)], axis=0)
        s_c = jnp.where(ok > 0, _nt(qs, cmp_ref[0, g]), NEG)
        p_c = jnp.exp2(s_c - jnp.max(s_c, axis=-1, keepdims=True)) * ok
        l_c = jnp.sum(p_c, axis=-1, keepdims=True)
        p_c = p_c * jnp.where(l_c > 0, 1.0 / l_c, 0.0)
        o_c = jnp.dot(p_c.astype(BF16), cmp_ref[0, G + g], preferred_element_type=F32)
        p_cat = jnp.concatenate([p_c[TILE * h:TILE * (h + 1)] for h in range(H)], axis=1).astype(BF16)
        imp = _nt(ovt_ref[...], p_cat)[HEAD_DIM:]
        nj = imp.shape[0]
        j = lax.broadcasted_iota(jnp.int32, (nj, TILE), 0)
        t = s0 + lax.broadcasted_iota(jnp.int32, (nj, TILE), 1)
        cur = t // NSA_SEL_LEN
        forced = jnp.where(j == 0, 1.0, 0.0) + jnp.where(j == cur, 1.0, 0.0) + jnp.where(j == cur - 1, 1.0, 0.0)
        imp = jnp.where(forced > 0, NSA_FORCE, imp)
        imp = jnp.where(j * NSA_SEL_LEN <= t, imp, NEG)
        rank = jnp.zeros((nj, TILE), F32)
        for k in range(nj):
            vk = imp[k:k + 1, :]
            rank = rank + jnp.where(j > k, jnp.where(vk >= imp, 1.0, 0.0), jnp.where(vk > imp, 1.0, 0.0))
        mneg = jnp.where(rank < n_top, 0.0, NEG)
        m_t = jnp.concatenate([jnp.zeros((LANE - nj, TILE), F32), mneg], axis=0)
        q_aug.append(qs + _stack_heads(m_t.T.astype(BF16), H))
        s_w = _nt(qs, kwin_ref[0, pl.ds(w0, wlen), LANE * g:LANE * (g + 1)])
        s_w = s_w + jnp.concatenate(
            [jnp.concatenate([twin_ref[H * g + h, widx[b]] for b in range(nb)], axis=1) for h in range(H)], axis=0)
        p_w = jnp.exp2(s_w - jnp.max(s_w, axis=-1, keepdims=True))
        o_w = jnp.dot(p_w.astype(BF16), vwin_ref[0, pl.ds(w0, wlen), LANE * g:LANE * (g + 1)],
                      preferred_element_type=F32)
        gt = gate_ref[0, :, LANE * g:LANE * (g + 1)]
        gate = lambda br: jnp.concatenate([gt[:, 3 * h + br:3 * h + br + 1] for h in range(H)], axis=0)
        o_fix.append((gate(0) * o_c + (gate(2) / o_w[:, HEAD_DIM:HEAD_DIM + 1]) * o_w, gate(1)))

    sub = tk // TILE

    def body(kc, carry):
        k0 = pl.multiple_of(kc * tk, tk)
        idx = [jnp.minimum(jnp.maximum(qi - (kc * sub + c) + 1, 0), nt - 1) for c in range(sub)]
        out = []
        for g in range(G):
            m = carry[g]
            s = _nt(q_aug[g], ksel_ref[0, pl.ds(k0, tk), LANE * g:LANE * (g + 1)])
            s = s + jnp.concatenate(
                [jnp.concatenate([tsel_ref[H * g + h, idx[c]] for c in range(sub)], axis=1) for h in range(H)], axis=0)
            m_new = jnp.maximum(m, jnp.max(s, axis=-1, keepdims=True))
            alpha = jnp.exp2(m - m_new)
            p = jnp.exp2(s - m_new)
            acc_ref[g] = acc_ref[g] * alpha + jnp.dot(
                p.astype(BF16), vsel_ref[0, pl.ds(k0, tk), LANE * g:LANE * (g + 1)], preferred_element_type=F32)
            out.append(m_new)
        return tuple(out)

    acc_ref[...] = jnp.zeros_like(acc_ref)
    lax.fori_loop(0, qi // sub + 1, body, (jnp.full((H * TILE, 1), NEG, F32),) * G)
    outs = []
    for g in range(G):
        acc = acc_ref[g]
        o = o_fix[g][0] + (o_fix[g][1] / acc[:, HEAD_DIM:HEAD_DIM + 1]) * acc
        outs += [o[TILE * h:TILE * (h + 1), :HEAD_DIM] for h in range(H)]
    o_ref[0] = jnp.concatenate(outs, axis=1).astype(BF16)


def _nsa_attention(q, ksel, vsel, kwin, vwin, cmp, gates, ovt, tsel, twin, tk):
    B, S, _ = q.shape
    n_q = S // TILE
    full = lambda a: pl.BlockSpec((1,) + a.shape[1:], lambda b, i: (b,) + (0,) * (a.ndim - 1))
    return pl.pallas_call(
        functools.partial(_nsa_kernel, tk=tk, n_cmp=S // NSA_CMP_STRIDE - 1,
                          n_top=min(NSA_TOP_N, S // NSA_SEL_LEN), nt=tsel.shape[1]),
        grid=(B, n_q),
        in_specs=[pl.BlockSpec((1, TILE, q.shape[2]), lambda b, i: (b, i, 0)),
                  full(ksel), full(vsel), full(kwin), full(vwin), full(cmp),
                  pl.BlockSpec((1, TILE, gates.shape[2]), lambda b, i: (b, i, 0)),
                  _resident(ovt.shape, lambda b, i: (0, 0)),
                  _resident(tsel.shape, lambda b, i: (0, 0, 0, 0)),
                  _resident(twin.shape, lambda b, i: (0, 0, 0, 0))],
        out_specs=pl.BlockSpec((1, TILE, HALF), lambda b, i: (b, i, 0)),
        out_shape=jax.ShapeDtypeStruct((B, S, HALF), BF16),
        scratch_shapes=[pltpu.VMEM((NSA_GROUPS, NSA_HPG * TILE, LANE), F32)],
        compiler_params=_cparams(2),
        name="nsa_attention",
    )(q, ksel, vsel, kwin, vwin, cmp, gates, ovt, tsel, twin)


def _overlap_t(S):
    n_cmp = (S - NSA_CMP_LEN) // NSA_CMP_STRIDE + 1
    n_sel = S // NSA_SEL_LEN
    cmp_idx = np.arange(n_cmp)[:, None] * NSA_CMP_STRIDE + np.arange(NSA_CMP_LEN)[None, :]
    ov = (cmp_idx[:, :, None] // NSA_SEL_LEN == np.arange(n_sel)[None, None, :]).mean(axis=1)
    out = np.zeros((LANE, S // NSA_CMP_STRIDE), np.float32)
    out[HEAD_DIM:HEAD_DIM + n_sel, :n_cmp] = ov.T
    return jnp.asarray(np.tile(out, (1, NSA_HPG)), BF16)


def _conformer_kernel(x_ref, w_ref, b_ref, g_ref, beta_ref, o_ref, buf, xs, *, tm):
    halo = 32

    @pl.when(pl.program_id(1) == 0)
    def _():
        buf[0:halo] = jnp.zeros((halo, buf.shape[1]), F32)

    buf[halo:halo + tm] = x_ref[0]
    n = tm + halo - 8
    for ph in range(1, 8):
        xs[ph, 0:n] = buf[pl.ds(ph, n), :]
    acc = jnp.zeros((tm, buf.shape[1]), F32) + b_ref[...]
    for j in range(CONV_WIDTH):
        off = halo - (CONV_WIDTH - 1) + j
        a, ph = off // 8, off % 8
        tap = buf[pl.ds(8 * a, tm), :] if ph == 0 else xs[ph, pl.ds(8 * a, tm), :]
        acc = acc + w_ref[j:j + 1, :] * tap
    mu = jnp.mean(acc, axis=-1, keepdims=True)
    var = jnp.mean(jnp.square(acc - mu), axis=-1, keepdims=True)
    y = (acc - mu) * lax.rsqrt(var + EPS) * g_ref[...] + beta_ref[...]
    o_ref[0] = (y * jax.nn.sigmoid(y)).astype(BF16)
    buf[0:halo] = buf[tm:tm + halo]


def _conformer(glu, w, b, g, beta, tm):
    B, S, C = glu.shape
    vec = lambda r: pl.BlockSpec((r, C), lambda bb, i: (0, 0))
    return pl.pallas_call(
        functools.partial(_conformer_kernel, tm=tm),
        grid=(B, S // tm),
        in_specs=[pl.BlockSpec((1, tm, C), lambda bb, i: (bb, i, 0)), vec(CONV_WIDTH), vec(1), vec(1), vec(1)],
        out_specs=pl.BlockSpec((1, tm, C), lambda bb, i: (bb, i, 0)),
        out_shape=jax.ShapeDtypeStruct((B, S, C), BF16),
        scratch_shapes=[pltpu.VMEM((tm + 32, C), F32), pltpu.VMEM((8, tm + 32, C), F32)],
        compiler_params=_cparams(2),
        name="conformer_conv",
    )(glu, w, b, g, beta)


def _even_out_kernel(x_ref, a_ref, u_ref, w_ref, o_ref):
    o_ref[0] = (x_ref[0] + jnp.dot(a_ref[0], w_ref[:HALF, :], preferred_element_type=F32)
                + jnp.dot(u_ref[0], w_ref[HALF:, :], preferred_element_type=F32))


def _even_out(x, a, u, w, tm):
    B, S, D = x.shape
    row = lambda n: pl.BlockSpec((1, tm, n), lambda b, i: (b, i, 0))
    return pl.pallas_call(
        _even_out_kernel, grid=(B, S // tm),
        in_specs=[row(D), row(HALF), row(HALF), _resident((D, D), lambda b, i: (0, 0))],
        out_specs=row(D), out_shape=jax.ShapeDtypeStruct((B, S, D), F32),
        compiler_params=_cparams(2), name="even_out",
    )(x, a, u, w)


def _odd_out_kernel(x_ref, c_ref, o1_ref, o4_ref, o16_ref, l1_ref, l4_ref, l16_ref, w_ref, o_ref,
                    b_o4, b_o16, b_l4, b_l16, *, tm):
    for src, dst, d in ((o4_ref, b_o4, 4), (l4_ref, b_l4, 4), (o16_ref, b_o16, 16), (l16_ref, b_l16, 16)):
        for r in range(d):
            blk = src[0, r].astype(F32)
            for c in range(HALF // LANE):
                dst[c, pl.ds(r, tm // d, stride=d), :] = blk[:, LANE * c:LANE * (c + 1)]
    full = lambda buf: jnp.concatenate([buf[c] for c in range(HALF // LANE)], axis=1)
    l1, l2, l3 = l1_ref[0, 0], full(b_l4), full(b_l16)
    m = jnp.maximum(jnp.maximum(l1, l2), l3)
    e1, e2, e3 = jnp.exp2(l1 - m), jnp.exp2(l2 - m), jnp.exp2(l3 - m)
    od = (e1 * o1_ref[0, 0] + e2 * full(b_o4) + e3 * full(b_o16)) / (e1 + e2 + e3)
    o_ref[0] = (x_ref[0] + jnp.dot(c_ref[0], w_ref[:HALF, :], preferred_element_type=F32)
                + jnp.dot(od.astype(BF16), w_ref[HALF:, :], preferred_element_type=F32))


def _odd_out(x, oc, os_, ls, w, tm):
    B, S, D = x.shape
    row = lambda n: pl.BlockSpec((1, tm, n), lambda b, i: (b, i, 0))
    dil = lambda d: pl.BlockSpec((1, d, tm // d, HALF), lambda b, i: (b, 0, i, 0))
    return pl.pallas_call(
        functools.partial(_odd_out_kernel, tm=tm), grid=(B, S // tm),
        in_specs=[row(D), row(HALF), dil(1), dil(4), dil(16), dil(1), dil(4), dil(16),
                  _resident((D, D), lambda b, i: (0, 0))],
        out_specs=row(D), out_shape=jax.ShapeDtypeStruct((B, S, D), F32),
        scratch_shapes=[pltpu.VMEM((HALF // LANE, tm, LANE), F32)] * 4,
        compiler_params=_cparams(2), name="odd_out",
    )(x, oc, *os_, *ls, w)


def _odd_proj_kernel(x_ref, g_ref, w_ref, cq_ref, ck_ref, cv_ref, q1_ref, k1_ref, v1_ref,
                     q4_ref, k4_ref, v4_ref, q16_ref, k16_ref, v16_ref, ybuf, *, tm):
    h = _rms_bf16(x_ref[0], g_ref[...])
    scale = Q_SCALE
    for n, (ref, s) in enumerate(((cq_ref, scale), (ck_ref, 1.0), (cv_ref, 1.0))):
        y = jnp.dot(h, w_ref[:, HALF * n:HALF * (n + 1)], preferred_element_type=F32)
        ref[0] = (y * s).astype(BF16)
    for n, (r1, r4, r16, s) in enumerate(((q1_ref, q4_ref, q16_ref, scale), (k1_ref, k4_ref, k16_ref, 1.0),
                                          (v1_ref, v4_ref, v16_ref, 1.0))):
        y = jnp.dot(h, w_ref[:, HALF * (n + 3):HALF * (n + 4)], preferred_element_type=F32) * s
        r1[0, 0] = y.astype(BF16)
        for c in range(HALF // LANE):
            ybuf[c] = y[:, LANE * c:LANE * (c + 1)]
        for d, ref in ((4, r4), (16, r16)):
            for r in range(d):
                ref[0, r] = jnp.concatenate([ybuf[c, pl.ds(r, tm // d, stride=d), :] for c in range(HALF // LANE)],
                                            axis=1).astype(BF16)


def _odd_proj(x, g, w, tm):
    B, S, D = x.shape
    row = lambda n: pl.BlockSpec((1, tm, n), lambda b, i: (b, i, 0))
    dil = lambda d: pl.BlockSpec((1, d, tm // d, HALF), lambda b, i: (b, 0, i, 0))
    dshape = lambda d: jax.ShapeDtypeStruct((B, d, S // d, HALF), BF16)
    return pl.pallas_call(
        functools.partial(_odd_proj_kernel, tm=tm), grid=(B, S // tm),
        in_specs=[row(D), _resident((1, D), lambda b, i: (0, 0)), _resident((D, 6 * HALF), lambda b, i: (0, 0))],
        out_specs=[row(HALF)] * 3 + [dil(1)] * 3 + [dil(4)] * 3 + [dil(16)] * 3,
        out_shape=[jax.ShapeDtypeStruct((B, S, HALF), BF16)] * 3 + [dshape(1)] * 3 + [dshape(4)] * 3 + [dshape(16)] * 3,
        scratch_shapes=[pltpu.VMEM((HALF // LANE, tm, LANE), F32)],
        compiler_params=_cparams(2), name="odd_proj",
    )(x, g, w)


def _diff_kernel(lp_ref, q_ref, k_ref, v_ref, t_ref, g_ref, o_ref, acc_ref, *, tk, lam_init, nt, nq):
    qi0 = pl.program_id(1) * nq
    H = DIFF_HEADS
    lp = lp_ref[...]
    lam = (jnp.exp(jnp.sum(lp[0:1] * lp[1:2], axis=-1, keepdims=True))
           - jnp.exp(jnp.sum(lp[2:3] * lp[3:4], axis=-1, keepdims=True)) + lam_init)
    lane = lax.broadcasted_iota(jnp.int32, (TILE, LANE), 1)
    qs = []
    for h in range(H):
        parts = []
        for u in range(nq):
            q = q_ref[0, pl.ds(u * TILE, TILE), LANE * h:LANE * (h + 1)]
            zero = jnp.zeros_like(q)
            parts += [jnp.where(lane < HEAD_DIM, q, zero), jnp.where(lane >= HEAD_DIM, q, zero)]
        qs.append(jnp.concatenate(parts, axis=0))
    sub = tk // TILE
    ones_col = jnp.where(lax.broadcasted_iota(jnp.int32, (tk, LANE), 1) == 0, 1.0, 0.0).astype(BF16)

    def body(kc, carry):
        k0 = pl.multiple_of(kc * tk, tk)
        out = []
        for h in range(H):
            m = carry[h]
            s = _nt(qs[h], k_ref[0, pl.ds(k0, tk), LANE * h:LANE * (h + 1)])
            rows = []
            for u in range(nq):
                bt = jnp.concatenate([t_ref[h, jnp.minimum(jnp.maximum(qi0 + u - (kc * sub + c) + 1, 0), nt - 1)]
                                      for c in range(sub)], axis=1)
                rows += [bt, bt]
            s = s + jnp.concatenate(rows, axis=0)
            m_new = jnp.maximum(m, jnp.max(s, axis=-1, keepdims=True))
            alpha = jnp.exp2(m - m_new)
            p = jnp.exp2(s - m_new)
            v_aug = jnp.concatenate([v_ref[0, pl.ds(k0, tk), LANE * h:LANE * (h + 1)], ones_col], axis=1)
            acc_ref[h] = acc_ref[h] * alpha + jnp.dot(p.astype(BF16), v_aug, preferred_element_type=F32)
            out.append(m_new)
        return tuple(out)

    acc_ref[...] = jnp.zeros_like(acc_ref)
    lax.fori_loop(0, (qi0 + nq - 1) // sub + 1, body, (jnp.full((2 * nq * TILE, 1), NEG, F32),) * H)
    for u in range(nq):
        outs = []
        for h in range(H):
            acc = acc_ref[h, pl.ds(2 * u * TILE, 2 * TILE), :]
            on = acc[:, :LANE] / acc[:, LANE:LANE + 1]
            o = on[:TILE] - lam * on[TILE:]
            o = o * lax.rsqrt(jnp.mean(o * o, axis=-1, keepdims=True) + EPS) * g_ref[...] * (1.0 - lam_init)
            outs.append(o.astype(BF16))
        o_ref[0, pl.ds(u * TILE, TILE), :] = jnp.concatenate(outs, axis=1)


def _diff_attention(lp, q, k, v, tiles, g, lam_init, tk, nq):
    B, S, _ = q.shape
    return pl.pallas_call(
        functools.partial(_diff_kernel, tk=tk, lam_init=lam_init, nt=tiles.shape[1], nq=nq),
        grid=(B, S // (nq * TILE)),
        in_specs=[pl.BlockSpec(lp.shape, lambda b, i: (0, 0)),
                  pl.BlockSpec((1, nq * TILE, HALF), lambda b, i: (b, i, 0)),
                  pl.BlockSpec((1, S, HALF), lambda b, i: (b, 0, 0)),
                  pl.BlockSpec((1, S, HALF), lambda b, i: (b, 0, 0)),
                  _resident(tiles.shape, lambda b, i: (0, 0, 0, 0)),
                  pl.BlockSpec(g.shape, lambda b, i: (0, 0))],
        out_specs=pl.BlockSpec((1, nq * TILE, HALF), lambda b, i: (b, i, 0)),
        out_shape=jax.ShapeDtypeStruct((B, S, HALF), BF16),
        scratch_shapes=[pltpu.VMEM((DIFF_HEADS, 2 * nq * TILE, 2 * LANE), F32)],
        compiler_params=_cparams(2), name="diff_attention",
    )(lp, q, k, v, tiles, g)


def _dilated_kernel(q_ref, k_ref, v_ref, t_ref, o_ref, lse_ref, *, nqb, nres):
    n0 = pl.program_id(2) * nqb
    H = 4
    W = H * HEAD_DIM
    lane = lax.broadcasted_iota(jnp.int32, (TILE, W), 1) // HEAD_DIM
    col = lax.broadcasted_iota(jnp.int32, (1, 2 * TILE), 1)
    for rr, qb in [(rr, qb) for rr in range(nres) for qb in range(nqb)]:
        n = n0 + qb
        cur = pl.multiple_of(n * TILE, TILE)
        prev = pl.multiple_of(jnp.maximum(n - 1, 0) * TILE, TILE)
        pmask = jnp.where((col < TILE) & (n == 0), NEG, 0.0)
        outs, lses = [], []
        for c in range(2):
            cs = slice(W * c, W * (c + 1))
            q = q_ref[0, rr, pl.ds(qb * TILE, TILE), cs]
            zero = jnp.zeros_like(q)
            qs = jnp.concatenate([jnp.where(lane == h, q, zero) for h in range(H)], axis=0)
            kcat = jnp.concatenate([k_ref[0, rr, pl.ds(prev, TILE), cs], k_ref[0, rr, pl.ds(cur, TILE), cs]], axis=0)
            vcat = jnp.concatenate([v_ref[0, rr, pl.ds(prev, TILE), cs], v_ref[0, rr, pl.ds(cur, TILE), cs]], axis=0)
            s = _nt(qs, kcat)
            s = s + jnp.concatenate(
                [jnp.concatenate([t_ref[H * c + h, 0], t_ref[H * c + h, 1]], axis=1) for h in range(H)], axis=0)
            s = s + pmask
            m = jnp.max(s, axis=-1, keepdims=True)
            p = jnp.exp2(s - m)
            l = jnp.sum(p, axis=-1, keepdims=True)
            r = jnp.dot(p.astype(BF16), vcat, preferred_element_type=F32) / l
            lse = m + jnp.log2(l)
            out = jnp.zeros((TILE, W), F32)
            lse_out = jnp.zeros((TILE, W), F32)
            for h in range(H):
                out = jnp.where(lane == h, r[TILE * h:TILE * (h + 1)], out)
                lse_out = jnp.where(lane == h, lse[TILE * h:TILE * (h + 1)], lse_out)
            outs.append(out.astype(BF16))
            lses.append(lse_out)
        o_ref[0, rr, pl.ds(qb * TILE, TILE), :] = jnp.concatenate(outs, axis=1)
        lse_ref[0, rr, pl.ds(qb * TILE, TILE), :] = jnp.concatenate(lses, axis=1)


def _dilated_group(q, k, v, tiles, n_blocks):
    B, dil, L, C = q.shape
    nqb = min(n_blocks, L // TILE)
    nres = min(n_blocks // nqb, dil)
    qb = pl.BlockSpec((1, nres, nqb * TILE, C), lambda b, r, n: (b, r, n, 0))
    kvb = pl.BlockSpec((1, nres, L, C), lambda b, r, n: (b, r, 0, 0))
    return pl.pallas_call(
        functools.partial(_dilated_kernel, nqb=nqb, nres=nres),
        grid=(B, dil // nres, L // (nqb * TILE)),
        in_specs=[qb, kvb, kvb, _resident(tiles.shape, lambda b, r, n: (0, 0, 0, 0))],
        out_specs=[qb, qb],
        out_shape=[jax.ShapeDtypeStruct((B, dil, L, C), BF16), jax.ShapeDtypeStruct((B, dil, L, C), F32)],
        compiler_params=_cparams(3), name=f"dilated_attention_d{dil}",
    )(q, k, v, tiles)


def _ffn_chunks():
    edges = list(range(0, D_FF, FF_CHUNK)) + [D_FF]
    return list(zip(edges[:-1], edges[1:]))


def _ffn_kernel(x_ref, g_ref, wg_ref, wu_ref, cw_ref, cb_ref, wd_ref, fg_ref, o_ref, gbuf, halo, *, tm, final):
    x = x_ref[0]
    h = _rms_bf16(x, g_ref[...])

    @pl.when(pl.program_id(1) == 0)
    def _():
        halo[...] = jnp.zeros_like(halo)

    chunks = _ffn_chunks()
    gate_up = lambda a, b: (jnp.dot(h, wg_ref[:, a:b], preferred_element_type=F32),
                            jnp.dot(h, wu_ref[:, a:b], preferred_element_type=F32))
    o_ref[0] = x
    nxt = gate_up(*chunks[0])
    for c, (a, b) in enumerate(chunks):
        w = b - a
        gg, uu = nxt
        if c + 1 < len(chunks):
            nxt = gate_up(*chunks[c + 1])
        gbuf[c % 2, 0:8, 0:w] = halo[c, :, 0:w]
        gbuf[c % 2, 8:8 + tm, 0:w] = gg
        halo[c, :, 0:w] = gg[tm - 8:tm]
        conv = (cw_ref[0:1, a:b] * gbuf[c % 2, pl.ds(6, tm), 0:w] + cw_ref[1:2, a:b] * gbuf[c % 2, pl.ds(7, tm), 0:w]
                + cw_ref[2:3, a:b] * gg + cb_ref[:, a:b])
        act = 0.5 * conv * (1.0 + lax.erf(conv * (2.0 ** -0.5))) * uu
        o_ref[0] += jnp.dot(act.astype(BF16), wd_ref[a:b, :], preferred_element_type=F32)
    if final:
        y = o_ref[0]
        o_ref[0] = y * lax.rsqrt(jnp.mean(y * y, axis=-1, keepdims=True) + EPS) * fg_ref[...]


def _ffn(x, g, wg, wu, cw, cb, wd, fg, tm, final):
    B, S, D = x.shape
    row = pl.BlockSpec((1, tm, D), lambda b, i: (b, i, 0))
    const = lambda shape: _resident(shape, lambda b, i: (0,) * len(shape))
    return pl.pallas_call(
        functools.partial(_ffn_kernel, tm=tm, final=final),
        grid=(B, S // tm),
        in_specs=[row, const((1, D)), const((D, D_FF)), const((D, D_FF)), const(cw.shape), const((1, D_FF)),
                  const((D_FF, D)), const((1, D))],
        out_specs=row, out_shape=jax.ShapeDtypeStruct((B, S, D), F32),
        scratch_shapes=[pltpu.VMEM((2, tm + 8, FF_CHUNK), F32), pltpu.VMEM((len(_ffn_chunks()), 8, FF_CHUNK), F32)],
        compiler_params=_cparams(2), name="conv_glu_ffn",
    )(x, g, wg, wu, cw, cb, wd, fg)


def kernel(x, rel_bias_table, mix_norm_g, ffn_norm_g, ffn_w_gate, ffn_w_up, ffn_conv_w, ffn_conv_b, ffn_w_down,
           ev_w_in, ev_w_out, nsa_phi_pos, nsa_phi_w1, nsa_phi_w2, conv_dw_w, conv_dw_b, conv_ln_g, conv_ln_b,
           od_w_in, od_w_out, diff_lambda, diff_norm_g, final_norm_g):
    B, S, D = x.shape
    depth = mix_norm_g.shape[0]
    assert D == D_MODEL and S % 2048 == 0 and S // NSA_SEL_LEN <= HEAD_DIM
    tm = 512
    tk = 512
    n_q = S // TILE
    NC = S // NSA_CMP_STRIDE

    tsel = _causal_tiles(rel_bias_table, REL_COL_NSA, NSA_GROUPS * NSA_HPG, n_q)
    twin = _window_tiles(rel_bias_table, REL_COL_NSA, NSA_GROUPS * NSA_HPG)
    tdiff = _causal_tiles(rel_bias_table, REL_COL_DIFF, DIFF_HEADS, n_q)
    tdil = [_dilated_tiles(rel_bias_table, REL_COL_DIL, DIL_HEADS, dil) for _, dil in DIL_PAIRS]
    ovt = _overlap_t(S)

    for layer in range(depth):
        g_mix = mix_norm_g[layer][None, :]
        if layer % 2 == 0:
            e = layer // 2
            q, kvc, ksel, vsel, kwin, vwin, gates, glu = _even_proj(x, g_mix, _even_proj_weights(ev_w_in[e]), tm)
            r = kvc.reshape(B, NC, NSA_CMP_STRIDE, 4, HEAD_DIM).transpose(0, 3, 1, 2, 4)
            r = r.reshape(B, 4, NC, NSA_CMP_STRIDE * HEAD_DIM)
            w2 = jnp.pad(nsa_phi_w2[e], ((0, 0), (0, 0), (0, LANE - HEAD_DIM))).astype(BF16)
            cmp = _compress(r, nsa_phi_pos[e].reshape(2, 1, NSA_CMP_LEN * HEAD_DIM), nsa_phi_w1[e].astype(BF16), w2)
            o_a = _nsa_attention(q, ksel, vsel, kwin, vwin, cmp, gates, ovt, tsel, twin, tk)
            u = _conformer(glu, conv_dw_w[e], conv_dw_b[e][None, :], conv_ln_g[e][None, :],
                           conv_ln_b[e][None, :], tm)
            x = _even_out(x, o_a, u, ev_w_out[e].astype(BF16), tm)
        else:
            o = layer // 2
            lam_init = 0.8 - 0.6 * math.exp(-0.3 * layer)
            cq, ck, cv, *dil_qkv = _odd_proj(x, g_mix, od_w_in[o].astype(BF16), tm)
            o_c = _diff_attention(diff_lambda[o], cq, ck, cv, tdiff, diff_norm_g[o][None, :], lam_init, tk, nq=4)
            outs, lses = zip(*[_dilated_group(*dil_qkv[3 * n:3 * n + 3], tdil[n], 8) for n in range(len(DIL_PAIRS))])
            x = _odd_out(x, o_c, outs, lses, od_w_out[o].astype(BF16), tm)
        x = _ffn(x, ffn_norm_g[layer][None, :], ffn_w_gate[layer].astype(BF16), ffn_w_up[layer].astype(BF16),
                 ffn_conv_w[layer], ffn_conv_b[layer][None, :], ffn_w_down[layer].astype(BF16),
                 final_norm_g[None, :], tm, final=(layer == depth - 1))
    return x
```
